```python
import math
import jax, jax.numpy as jnp
from jax import lax
import numpy as np

D_MODEL = 1024
BATCH = 8
SEQ = 2048
DEPTH = 2
DEC_BATCH = 128
DEC_SEQ = 1
PAST_LEN = 16384
PAGE_SIZE = 128

N_A_LAYERS = DEPTH // 2
N_B_LAYERS = DEPTH - N_A_LAYERS
HGRN_HEADS = 8
HGRN_DK = D_MODEL // HGRN_HEADS
HGRN_DV = D_MODEL // HGRN_HEADS
HGRN_CHUNK = 64
MLA_HEADS = 16
QK_NOPE = 64
QK_ROPE = 32
V_DIM = 64
KV_LORA = 256
Q_LORA = 384
ROPE_THETA = 10000.0
MLA_SCALE = (QK_NOPE + QK_ROPE) ** -0.5
Q_BLOCK = 128
PAGES_PER_STEP = 16
PEER_HEADS = 8
PEER_KEY_DIM = 256
N_KEYS = 128
N_EXPERTS = N_KEYS * N_KEYS
PEER_TOPK = 16
PEER_TOKEN_BLOCK = 128
DEEPNORM_ALPHA = (2.0 * DEPTH) ** 0.25
DEEPNORM_BETA = (8.0 * DEPTH) ** -0.25
LN_EPS = 1e-5
RMS_EPS = 1e-6

kernel_name = 'yoco_hgrn2_mla_peer_step'


def _layernorm(x, g, b):
    xf = x.astype(jnp.float32)
    mu = xf.mean(-1, keepdims=True)
    var = jnp.square(xf - mu).mean(-1, keepdims=True)
    return ((xf - mu) * lax.rsqrt(var + LN_EPS) * g + b).astype(x.dtype)


def _rmsnorm(x, g):
    xf = x.astype(jnp.float32)
    return (xf * lax.rsqrt(jnp.mean(xf * xf, -1, keepdims=True) + RMS_EPS) * g).astype(x.dtype)


def _rope(x, pos):
    half = QK_ROPE // 2
    inv = ROPE_THETA ** (-jnp.arange(half, dtype=jnp.float32) / half)
    ang = pos.astype(jnp.float32)[:, None] * inv[None, :]
    shape = (1, ang.shape[0]) + (1,) * (x.ndim - 3) + (half,)
    cos = jnp.cos(ang).reshape(shape)
    sin = jnp.sin(ang).reshape(shape)
    xf = x.astype(jnp.float32)
    x1, x2 = xf[..., :half], xf[..., half:]
    return jnp.concatenate([x1 * cos - x2 * sin, x2 * cos + x1 * sin], axis=-1).astype(x.dtype)


def _gla_chunked(q, k, v, logf, S0):
    B, T, H, _ = q.shape
    DV = v.shape[-1]
    C = min(HGRN_CHUNK, T)
    n = -(-T // C)
    pad = n * C - T

    def prep(a):
        a = jnp.pad(a.astype(jnp.float32), ((0, 0), (0, pad), (0, 0), (0, 0)))
        return a.reshape(B, n, C, H, a.shape[-1]).transpose(1, 0, 3, 2, 4)

    xs = (prep(q), prep(k), prep(v), prep(logf))
    tri = jnp.tril(jnp.ones((C, C), dtype=bool))[None, None, :, :, None]

    def step(S, inp):
        qc, kc, vc, gc = inp
        b = jnp.cumsum(gc, axis=2)
        o_inter = jnp.einsum('bhtd,bhde->bhte', qc * jnp.exp(b), S)
        diff = jnp.where(tri, b[:, :, :, None, :] - b[:, :, None, :, :], -jnp.inf)
        att = jnp.einsum('bhtd,bhsd,bhtsd->bhts', qc, kc, jnp.exp(diff))
        o = o_inter + jnp.einsum('bhts,bhse->bhte', att, vc)
        b_last = b[:, :, -1:, :]
        S_new = jnp.exp(b_last[:, :, 0, :])[..., None] * S + jnp.einsum(
            'bhsd,bhse->bhde', kc * jnp.exp(b_last - b), vc)
        return S_new, o

    S_fin, o = lax.scan(step, S0.astype(jnp.float32), xs)
    o = o.transpose(1, 0, 3, 2, 4).reshape(B, n * C, H, DV)[:, :T]
    return o, S_fin


def _hgrn2_mixer(x, S0, w_in, lb, norm_g, w_out):
    B, T, _ = x.shape
    q, fz, i, g = jnp.split(x @ w_in, 4, axis=-1)
    lbh = lb.reshape(HGRN_HEADS, HGRN_DK)
    fz = fz.reshape(B, T, HGRN_HEADS, HGRN_DK).astype(jnp.float32)
    logf = jnp.log(lbh + (1.0 - lbh) * jax.nn.sigmoid(fz))
    k = (1.0 - lbh) * jax.nn.sigmoid(-fz)
    o, S = _gla_chunked(q.reshape(B, T, HGRN_HEADS, HGRN_DK), k, i.reshape(B, T, HGRN_HEADS, HGRN_DV), logf, S0)
    o = _rmsnorm(o, norm_g).astype(x.dtype) * jax.nn.silu(g.reshape(B, T, HGRN_HEADS, HGRN_DV))
    return o.reshape(B, T, HGRN_HEADS * HGRN_DV) @ w_out, S


def _peer(x, w_q, k1, k2, u, v):
    B, T, D = x.shape
    n = B * T
    blk = min(PEER_TOKEN_BLOCK, n)
    nb = -(-n // blk)
    xp = jnp.pad(x.reshape(n, D), ((0, nb * blk - n), (0, 0))).reshape(nb, blk, D)
    half = PEER_KEY_DIM // 2

    def block(xb):
        q = (xb @ w_q).reshape(blk, PEER_HEADS, PEER_KEY_DIM)
        s1 = jnp.einsum('thd,hnd->thn', q[..., :half], k1).astype(jnp.float32)
        s2 = jnp.einsum('thd,hnd->thn', q[..., half:], k2).astype(jnp.float32)
        v1, i1 = lax.top_k(s1, PEER_TOPK)
        v2, i2 = lax.top_k(s2, PEER_TOPK)
        cand = (v1[..., :, None] + v2[..., None, :]).reshape(blk, PEER_HEADS, PEER_TOPK * PEER_TOPK)
        cidx = (i1[..., :, None] * N_KEYS + i2[..., None, :]).reshape(blk, PEER_HEADS, PEER_TOPK * PEER_TOPK)
        sv, si = lax.top_k(cand, PEER_TOPK)
        e = jnp.take_along_axis(cidx, si, axis=-1)
        gate = jax.nn.softmax(sv, axis=-1)
        act = jax.nn.gelu(jnp.einsum('thkd,td->thk', u[e], xb).astype(jnp.float32), approximate=False)
        return jnp.einsum('thk,thkd->td', (gate * act).astype(x.dtype), v[e])

    y = lax.map(block, xp).reshape(nb * blk, D)[:n]
    return y.reshape(B, T, D)


def _mla_kv(x, pos, w_dkv, kv_norm_g):
    kv = x @ w_dkv
    return _rmsnorm(kv[..., :KV_LORA], kv_norm_g), _rope(kv[..., KV_LORA:], pos)


def _mla_q(x, pos, w_dq, q_norm_g, w_uq, w_uk):
    B, T, _ = x.shape
    q = (_rmsnorm(x @ w_dq, q_norm_g) @ w_uq).reshape(B, T, MLA_HEADS, QK_NOPE + QK_ROPE)
    q_pe = _rope(q[..., QK_NOPE:], pos)
    q_lat = jnp.einsum('bthn,lhn->bthl', q[..., :QK_NOPE], w_uk)
    return q_lat, q_pe


def _mla_scores(q_lat, q_pe, ckv, kpe):
    s = jnp.einsum('bthl,bkl->bhtk', q_lat, ckv, preferred_element_type=jnp.float32)
    s = s + jnp.einsum('bthr,bkr->bhtk', q_pe, kpe, preferred_element_type=jnp.float32)
    return s * MLA_SCALE


def _mla_attend_prompt(q_lat, q_pe, ckv, kpe):
    B, S, H, L = q_lat.shape
    qb = min(Q_BLOCK, S)
    nb = S // qb
    kpos = jnp.arange(S)
    ckv32 = ckv.astype(jnp.float32)
    ql = q_lat.reshape(B, nb, qb, H, L).transpose(1, 0, 2, 3, 4)
    qp = q_pe.reshape(B, nb, qb, H, QK_ROPE).transpose(1, 0, 2, 3, 4)

    def block(inp):
        qlb, qpb, i = inp
        qpos = i * qb + jnp.arange(qb)
        s = jnp.where(kpos[None, :] <= qpos[:, None], _mla_scores(qlb, qpb, ckv, kpe), -jnp.inf)
        p = jax.nn.softmax(s, axis=-1)
        return jnp.einsum('bhtk,bkl->bthl', p, ckv32)

    o = lax.map(block, (ql, qp, jnp.arange(nb)))
    return o.transpose(1, 0, 2, 3, 4).reshape(B, S, H, L).astype(q_lat.dtype)


def _online_update(carry, s, vals):
    m, l, acc = carry
    m_new = jnp.maximum(m, s.max(-1))
    corr = jnp.exp(m - m_new)
    p = jnp.exp(s - m_new[..., None])
    l = l * corr + p.sum(-1)
    acc = acc * corr[..., None] + jnp.einsum('bhtk,bkl->bhtl', p, vals.astype(jnp.float32))
    return m_new, l, acc


def _pages_per_step(n_pages):
    return max(d for d in range(1, PAGES_PER_STEP + 1) if n_pages % d == 0)


def _mla_attend_sample(q_lat, q_pe, ckv_new, kpe_new, cache_ckv, cache_kpe, page_table):
    Bd, T, H, L = q_lat.shape
    n_pages = page_table.shape[1]
    g = _pages_per_step(n_pages)
    pages = page_table.reshape(Bd, n_pages // g, g).transpose(1, 0, 2)
    init = (jnp.full((Bd, H, T), -1e30, jnp.float32),
            jnp.zeros((Bd, H, T), jnp.float32),
            jnp.zeros((Bd, H, T, L), jnp.float32))

    def step(carry, pg):
        ck = cache_ckv[pg].reshape(Bd, g * cache_ckv.shape[1], L)
        kp = cache_kpe[pg].reshape(Bd, g * cache_kpe.shape[1], QK_ROPE)
        return _online_update(carry, _mla_scores(q_lat, q_pe, ck, kp), ck), None

    carry, _ = lax.scan(step, init, pages)
    causal = jnp.tril(jnp.ones((T, T), dtype=bool))
    s_new = jnp.where(causal, _mla_scores(q_lat, q_pe, ckv_new, kpe_new), -jnp.inf)
    m, l, acc = _online_update(carry, s_new, ckv_new)
    return (acc / l[..., None]).transpose(0, 2, 1, 3).astype(q_lat.dtype)


def _mla_out(o_lat, w_uv, w_o):
    B, T, _, _ = o_lat.shape
    o = jnp.einsum('bthl,lhv->bthv', o_lat, w_uv)
    return o.reshape(B, T, MLA_HEADS * V_DIM) @ w_o


def setup_inputs(seed: int = 0) -> dict:
    key = jax.random.key(seed)
    ks = jax.random.split(key, 26)
    f32 = jnp.float32

    def nrm(k, shape, scale=1.0):
        return jax.random.normal(k, shape, f32) * scale

    n_pages = PAST_LEN // PAGE_SIZE
    n_used = DEC_BATCH * n_pages
    n_pool = n_used + -(-n_used // 4)
    page_table = jax.random.permutation(ks[0], n_pool)[:n_used].reshape(DEC_BATCH, n_pages).astype(jnp.int32)

    hd = HGRN_HEADS * HGRN_DK
    col_scale = jnp.concatenate([jnp.ones((2 * hd,), f32), jnp.full((hd,), DEEPNORM_BETA, f32), jnp.ones((hd,), f32)])
    hgrn_w_in = nrm(ks[1], (N_A_LAYERS, D_MODEL, 4 * hd), D_MODEL ** -0.5) * col_scale
    ukv_scale = jnp.concatenate([jnp.ones((QK_NOPE,), f32), jnp.full((V_DIM,), DEEPNORM_BETA, f32)])
    mla_w_ukv = nrm(ks[2], (KV_LORA, MLA_HEADS, QK_NOPE + V_DIM), KV_LORA ** -0.5) * ukv_scale

    return {
        'x_prompt': nrm(ks[3], (BATCH, SEQ, D_MODEL)),
        'x_sample': nrm(ks[4], (DEC_BATCH, DEC_SEQ, D_MODEL)),
        'state_hgrn': nrm(ks[5], (N_A_LAYERS, DEC_BATCH, HGRN_HEADS, HGRN_DK, HGRN_DV), 0.5),
        'cache_ckv': nrm(ks[6], (n_pool, PAGE_SIZE, KV_LORA)),
        'cache_kpe': nrm(ks[7], (n_pool, PAGE_SIZE, QK_ROPE)),
        'page_table': page_table,
        'hgrn_w_in': hgrn_w_in,
        'hgrn_lb_logits': nrm(ks[8], (N_A_LAYERS + 1, hd), 0.5),
        'hgrn_norm_g': 1.0 + nrm(ks[9], (N_A_LAYERS, HGRN_HEADS, HGRN_DV), 0.02),
        'hgrn_w_out': nrm(ks[10], (N_A_LAYERS, HGRN_HEADS * HGRN_DV, D_MODEL), (HGRN_HEADS * HGRN_DV) ** -0.5 * DEEPNORM_BETA),
        'mla_w_dkv': nrm(ks[11], (D_MODEL, KV_LORA + QK_ROPE), D_MODEL ** -0.5),
        'mla_kv_norm_g': 1.0 + nrm(ks[12], (KV_LORA,), 0.02),
        'mla_w_ukv': mla_w_ukv,
        'mla_w_dq': nrm(ks[13], (N_B_LAYERS, D_MODEL, Q_LORA), D_MODEL ** -0.5),
        'mla_q_norm_g': 1.0 + nrm(ks[14], (N_B_LAYERS, Q_LORA), 0.02),
        'mla_w_uq': nrm(ks[15], (N_B_LAYERS, Q_LORA, MLA_HEADS * (QK_NOPE + QK_ROPE)), Q_LORA ** -0.5),
        'mla_w_o': nrm(ks[16], (N_B_LAYERS, MLA_HEADS * V_DIM, D_MODEL), (MLA_HEADS * V_DIM) ** -0.5 * DEEPNORM_BETA),
        'peer_w_q': nrm(ks[17], (DEPTH, D_MODEL, PEER_HEADS * PEER_KEY_DIM), D_MODEL ** -0.5),
        'peer_k1': nrm(ks[18], (DEPTH, PEER_HEADS, N_KEYS, PEER_KEY_DIM // 2), (PEER_KEY_DIM // 2) ** -0.5),
        'peer_k2': nrm(ks[19], (DEPTH, PEER_HEADS, N_KEYS, PEER_KEY_DIM // 2), (PEER_KEY_DIM // 2) ** -0.5),
        'peer_u': nrm(ks[20], (DEPTH, N_EXPERTS, D_MODEL), D_MODEL ** -0.5 * DEEPNORM_BETA),
        'peer_v': nrm(ks[21], (DEPTH, N_EXPERTS, D_MODEL), DEEPNORM_BETA),
        'ln_g': 1.0 + nrm(ks[22], (DEPTH, 2, D_MODEL), 0.02),
        'ln_b': nrm(ks[23], (DEPTH, 2, D_MODEL), 0.02),
    }


def reference(x_prompt, x_sample, state_hgrn, cache_ckv, cache_kpe, page_table,
              hgrn_w_in, hgrn_lb_logits, hgrn_norm_g, hgrn_w_out,
              mla_w_dkv, mla_kv_norm_g, mla_w_ukv,
              mla_w_dq, mla_q_norm_g, mla_w_uq, mla_w_o,
              peer_w_q, peer_k1, peer_k2, peer_u, peer_v,
              ln_g, ln_b):
    lower_bounds = jnp.cumsum(jax.nn.softmax(hgrn_lb_logits.astype(jnp.float32), axis=0), axis=0)
    w_uk = mla_w_ukv[..., :QK_NOPE]
    w_uv = mla_w_ukv[..., QK_NOPE:]

    def trunk(x, pos, hgrn_init, attend):
        new_states = []
        ckv = None
        kpe = None
        for l in range(DEPTH):
            if l < N_A_LAYERS:
                h, S = _hgrn2_mixer(x, hgrn_init(l), hgrn_w_in[l], lower_bounds[l], hgrn_norm_g[l], hgrn_w_out[l])
                new_states.append(S)
            else:
                j = l - N_A_LAYERS
                if j == 0:
                    ckv, kpe = _mla_kv(x, pos, mla_w_dkv, mla_kv_norm_g)
                q_lat, q_pe = _mla_q(x, pos, mla_w_dq[j], mla_q_norm_g[j], mla_w_uq[j], w_uk)
                h = _mla_out(attend(q_lat, q_pe, ckv, kpe), w_uv, mla_w_o[j])
            x = _layernorm(DEEPNORM_ALPHA * x + h, ln_g[l, 0], ln_b[l, 0])
            f = _peer(x, peer_w_q[l], peer_k1[l], peer_k2[l], peer_u[l], peer_v[l])
            x = _layernorm(DEEPNORM_ALPHA * x + f, ln_g[l, 1], ln_b[l, 1])
        return x, jnp.stack(new_states), ckv, kpe

    pos_p = jnp.arange(x_prompt.shape[1])
    y_prompt, st_p, ckv_p, kpe_p = trunk(
        x_prompt, pos_p,
        lambda l: jnp.zeros((x_prompt.shape[0], HGRN_HEADS, HGRN_DK, HGRN_DV), jnp.float32),
        _mla_attend_prompt)

    pos_s = PAST_LEN + jnp.arange(x_sample.shape[1])
    y_sample, st_s, ckv_s, kpe_s = trunk(
        x_sample, pos_s,
        lambda l: state_hgrn[l],
        lambda ql, qp, ck, kp: _mla_attend_sample(ql, qp, ck, kp, cache_ckv, cache_kpe, page_table))

    return (y_prompt, y_sample, st_p, st_s, ckv_p, kpe_p, ckv_s, kpe_s)
```

```python
import functools
import math

import numpy as np
import jax
import jax.numpy as jnp
from jax import lax
from jax.experimental import pallas as pl
from jax.experimental.pallas import tpu as pltpu

F32 = jnp.float32
BF16 = jnp.bfloat16
I32 = jnp.int32

D_MODEL = 1024
DEPTH = 2
N_A_LAYERS = DEPTH // 2
HGRN_HEADS = 8
HGRN_DK = 128
HGRN_DV = 128
MLA_HEADS = 16
QK_NOPE = 64
QK_ROPE = 32
V_DIM = 64
KV_LORA = 256
Q_LORA = 384
ROPE_THETA = 10000.0
MLA_SCALE = (QK_NOPE + QK_ROPE) ** -0.5
PAGE_SIZE = 128
PEER_HEADS = 8
PEER_KEY_DIM = 256
N_KEYS = 128
PEER_TOPK = 16
DEEPNORM_ALPHA = (2.0 * DEPTH) ** 0.25
LN_EPS = 1e-5
RMS_EPS = 1e-6

LANES = 128
SUBLANES = 8
VMEM_LIMIT_BYTES = 56 * 1024 * 1024


def _params(sem):
    return pltpu.CompilerParams(dimension_semantics=sem, vmem_limit_bytes=VMEM_LIMIT_BYTES)


def _dot(a, b):
    return jnp.dot(a, b, preferred_element_type=F32)


def _dot_nt(a, b):
    return lax.dot_general(a, b, (((1,), (1,)), ((), ())), preferred_element_type=F32)


def _dot_tn(a, b):
    return lax.dot_general(a, b, (((0,), (0,)), ((), ())), preferred_element_type=F32)


def _layernorm(x, g, b):
    mu = jnp.mean(x, axis=-1, keepdims=True)
    xc = x - mu
    var = jnp.mean(xc * xc, axis=-1, keepdims=True)
    return xc * lax.rsqrt(var + LN_EPS) * g + b


def _rmsnorm(x, g):
    return x * lax.rsqrt(jnp.mean(x * x, axis=-1, keepdims=True) + RMS_EPS) * g


def _mm_kernel(x_ref, w_ref, o_ref):
    o_ref[...] = _dot(x_ref[...].astype(BF16), w_ref[...]).astype(o_ref.dtype)


def _matmul(x, w, tm, tn, out_dtype=F32):
    m, k = x.shape
    n = w.shape[1]
    return pl.pallas_call(
        _mm_kernel,
        grid=(m // tm, n // tn),
        in_specs=[pl.BlockSpec((tm, k), lambda i, j: (i, 0)),
                  pl.BlockSpec((k, tn), lambda i, j: (0, j))],
        out_specs=pl.BlockSpec((tm, tn), lambda i, j: (i, j)),
        out_shape=jax.ShapeDtypeStruct((m, n), out_dtype),
        compiler_params=_params(("parallel", "parallel")),
        name="matmul",
    )(x, w)


def _mm_ln_kernel(*refs, two_stage):
    if two_stage:
        h_ref, w0_ref, w_ref, x_ref, g_ref, b_ref, o_ref = refs
        h = _dot(h_ref[...].astype(BF16), w0_ref[...]).astype(BF16)
    else:
        h_ref, w_ref, x_ref, g_ref, b_ref, o_ref = refs
        h = h_ref[...].astype(BF16)
    y = DEEPNORM_ALPHA * x_ref[...] + _dot(h, w_ref[...])
    o_ref[...] = _layernorm(y, g_ref[...], b_ref[...])


def _matmul_residual_ln(h, w, x, g, b, tm, w0=None):
    m, k = h.shape
    d = w.shape[1]
    row = lambda i: (i, 0)
    fixed = lambda i: (0, 0)
    ins = [h] + ([w0] if w0 is not None else []) + [w, x, g.reshape(1, d), b.reshape(1, d)]
    specs = [pl.BlockSpec((tm, k), row)]
    if w0 is not None:
        specs.append(pl.BlockSpec(w0.shape, fixed))
    specs += [pl.BlockSpec(w.shape, fixed), pl.BlockSpec((tm, d), row),
              pl.BlockSpec((1, d), fixed), pl.BlockSpec((1, d), fixed)]
    return pl.pallas_call(
        functools.partial(_mm_ln_kernel, two_stage=w0 is not None),
        grid=(m // tm,),
        in_specs=specs,
        out_specs=pl.BlockSpec((tm, d), row),
        out_shape=jax.ShapeDtypeStruct((m, d), F32),
        compiler_params=_params(("parallel",)),
        name="matmul_residual_ln",
    )(*ins)


def _hgrn_level_matrix(c):
    t = np.arange(c)[:, None]
    s = np.arange(c)[None, :]
    mats = [(s <= t)]
    m = c // 2
    while m >= 1:
        bd = (t // (2 * m)) * (2 * m) + m - 1
        mats.append(s <= bd)
        m //= 2
    return np.concatenate(mats, axis=0).astype(np.float32)


def _hgrn_kernel(*refs, c, n_real, layer, has_s0):
    if has_s0:
        (q_ref, f_ref, i_ref, g_ref, lbl_ref, ng_ref, lvl_ref, s0_ref, o_ref, s_ref, st_ref) = refs
    else:
        (q_ref, f_ref, i_ref, g_ref, lbl_ref, ng_ref, lvl_ref, o_ref, s_ref, st_ref) = refs
    h = pl.program_id(1)
    ci = pl.program_id(2)
    dk = q_ref.shape[-1]

    @pl.when(ci == 0)
    def _():
        if has_s0:
            st_ref[...] = s0_ref[...].T
        else:
            st_ref[...] = jnp.zeros_like(st_ref)

    rows = q_ref.shape[0]
    q = jnp.broadcast_to(q_ref[...], (c, dk)) if rows != c else q_ref[...]
    fz = jnp.broadcast_to(f_ref[...], (c, dk)) if rows != c else f_ref[...]
    v = jnp.broadcast_to(i_ref[...], (c, dk)) if rows != c else i_ref[...]

    lbl = lbl_ref[...]
    e = jnp.exp(lbl - jnp.max(lbl, axis=0, keepdims=True))
    sm = e / jnp.sum(e, axis=0, keepdims=True)
    lb = jnp.sum(sm[:layer + 1], axis=0, keepdims=True)

    logf = jnp.log(lb + (1.0 - lb) * jax.nn.sigmoid(fz))
    k = (1.0 - lb) * jax.nn.sigmoid(-fz)
    row_id = lax.broadcasted_iota(I32, (c, 1), 0)
    if n_real != c:
        live = row_id < n_real
        logf = jnp.where(live, logf, 0.0)
        k = jnp.where(live, k, 0.0)

    hi = logf.astype(BF16)
    r1 = logf - hi.astype(F32)
    mid = r1.astype(BF16)
    lo = (r1 - mid.astype(F32)).astype(BF16)
    parts = _dot(lvl_ref[...], jnp.concatenate([hi, mid, lo], axis=1))
    sums = parts[:, :dk] + (parts[:, dk:2 * dk] + parts[:, 2 * dk:])
    b = sums[:c]

    kb = k.astype(BF16)
    vb = v.astype(BF16)
    col_id = lax.broadcasted_iota(I32, (1, c), 1)
    att = jnp.where(row_id == col_id, _dot_nt(q.astype(BF16), kb), 0.0)
    m = c // 2
    lvl = 1
    while m >= 1:
        r = sums[lvl * c:(lvl + 1) * c]
        second = (row_id & (2 * m - 1)) >= m
        ql = q * jnp.exp(jnp.where(second, b - r, -jnp.inf))
        kl = k * jnp.exp(jnp.where(second, -jnp.inf, r - b))
        a = _dot_nt(ql.astype(BF16), kl.astype(BF16))
        if 2 * m == c:
            att = att + a
        else:
            sh = int(math.log2(2 * m))
            att = att + jnp.where((row_id >> sh) == (col_id >> sh), a, 0.0)
        m //= 2
        lvl += 1

    st = st_ref[...]
    o = _dot_nt((q * jnp.exp(b)).astype(BF16), st.astype(BF16)) + _dot(att.astype(BF16), vb)
    b_last = b[c - 1:c, :]
    st_new = st * jnp.exp(b_last) + _dot_tn(vb, (k * jnp.exp(b_last - b)).astype(BF16))
    st_ref[...] = st_new

    ng = ng_ref[pl.ds(h, 1), :]
    g = g_ref[...]
    o = _rmsnorm(o, ng)
    if rows != c:
        o = o[:rows]
    o_ref[...] = o * (g * jax.nn.sigmoid(g))

    @pl.when(ci == pl.num_programs(2) - 1)
    def _():
        s_ref[...] = st_new.T


def _hgrn_mixer(proj, batch, t, lb_logits, norm_g, s0, layer):
    n = proj.shape[0]
    hd = HGRN_HEADS * HGRN_DK
    nh = HGRN_HEADS
    if t >= LANES:
        c, rows, n_real = LANES, LANES, LANES
        arr = proj
        col = lambda off: pl.BlockSpec((rows, HGRN_DK), lambda b, h, ci: (b * (t // c) + ci, off * nh + h))
        o_spec = pl.BlockSpec((rows, HGRN_DV), lambda b, h, ci: (b * (t // c) + ci, h))
        o_shape = jax.ShapeDtypeStruct((n, nh * HGRN_DV), F32)
        nchunk = t // c
    else:
        assert t == 1
        c, rows, n_real = 2 * SUBLANES, 1, 1
        arr = proj.reshape(n, 1, 4 * hd)
        col = lambda off: pl.BlockSpec((None, 1, HGRN_DK), lambda b, h, ci: (b, 0, off * nh + h))
        o_spec = pl.BlockSpec((None, 1, HGRN_DV), lambda b, h, ci: (b, 0, h))
        o_shape = jax.ShapeDtypeStruct((n, 1, nh * HGRN_DV), F32)
        nchunk = 1
    lvl = jnp.asarray(_hgrn_level_matrix(c), BF16)
    nl = lb_logits.shape[0]
    ins = [arr, arr, arr, arr, lb_logits, norm_g, lvl]
    specs = [col(0), col(1), col(2), col(3),
             pl.BlockSpec((nl, HGRN_DK), lambda b, h, ci: (0, h)),
             pl.BlockSpec((nh, HGRN_DV), lambda b, h, ci: (0, 0)),
             pl.BlockSpec(lvl.shape, lambda b, h, ci: (0, 0))]
    if s0 is not None:
        ins.append(s0)
        specs.append(pl.BlockSpec((None, None, HGRN_DK, HGRN_DV), lambda b, h, ci: (b, h, 0, 0)))
    o, s = pl.pallas_call(
        functools.partial(_hgrn_kernel, c=c, n_real=n_real, layer=layer, has_s0=s0 is not None),
        grid=(batch, nh, nchunk),
        in_specs=specs,
        out_specs=[o_spec, pl.BlockSpec((None, None, HGRN_DK, HGRN_DV), lambda b, h, ci: (b, h, 0, 0))],
        out_shape=[o_shape, jax.ShapeDtypeStruct((batch, nh, HGRN_DK, HGRN_DV), F32)],
        scratch_shapes=[pltpu.VMEM((HGRN_DV, HGRN_DK), F32)],
        compiler_params=_params(("parallel", "parallel", "arbitrary")),
        name="hgrn2_chunk",
    )(*ins)
    return o.reshape(n, nh * HGRN_DV), s


def _top16(s, val_ref, idx_ref):
    n = s.shape[0]
    rid = lax.broadcasted_iota(I32, s.shape, 0)
    rank = jnp.full(s.shape, PEER_TOPK, I32)
    for kk in range(PEER_TOPK):
        mx = jnp.max(s, axis=0, keepdims=True)
        ix = jnp.min(jnp.where(s == mx, rid, n), axis=0, keepdims=True)
        hit = rid == ix
        val_ref[kk:kk + 1, :] = mx
        idx_ref[kk:kk + 1, :] = ix
        rank = jnp.where(hit, kk, rank)
        s = jnp.where(hit, -jnp.inf, s)
    return rank


def _peer_topk_kernel(q_ref, k1_ref, k2_ref, inv1_ref, inv2_ref, p_ref,
                      v1_ref, i1_ref, v2_ref, i2_ref, sv_ref, si_ref, cand_ref):
    half = PEER_KEY_DIM // 2
    kk = PEER_TOPK
    for h in range(PEER_HEADS):
        q = q_ref[:, h * PEER_KEY_DIM:(h + 1) * PEER_KEY_DIM].astype(BF16)
        s1 = _dot_nt(k1_ref[h], q[:, :half])
        s2 = _dot_nt(k2_ref[h], q[:, half:])
        inv1_ref[h] = _top16(s1, v1_ref, i1_ref)
        inv2_ref[h] = _top16(s2, v2_ref, i2_ref)
        v2 = v2_ref[...]
        for i in range(kk):
            cand_ref[i * kk:(i + 1) * kk, :] = v1_ref[i:i + 1, :] + v2
        rank = _top16(cand_ref[...], sv_ref, si_ref)
        sv = sv_ref[...]
        e = jnp.exp(sv - jnp.max(sv, axis=0, keepdims=True))
        gate = e / jnp.sum(e, axis=0, keepdims=True)
        p = jnp.zeros(rank.shape, F32)
        for j in range(kk):
            p = jnp.where(rank == j, gate[j:j + 1, :], p)
        p_ref[h] = p


def _peer_topk(q, k1, k2, tt):
    n = q.shape[0]
    nh = PEER_HEADS
    kk = PEER_TOPK
    return pl.pallas_call(
        _peer_topk_kernel,
        grid=(n // tt,),
        in_specs=[pl.BlockSpec((tt, nh * PEER_KEY_DIM), lambda i: (i, 0)),
                  pl.BlockSpec(k1.shape, lambda i: (0, 0, 0)),
                  pl.BlockSpec(k2.shape, lambda i: (0, 0, 0))],
        out_specs=[pl.BlockSpec((nh, N_KEYS, tt), lambda i: (0, 0, i)),
                   pl.BlockSpec((nh, N_KEYS, tt), lambda i: (0, 0, i)),
                   pl.BlockSpec((nh, kk * kk, tt), lambda i: (0, 0, i))],
        out_shape=[jax.ShapeDtypeStruct((nh, N_KEYS, n), I32),
                   jax.ShapeDtypeStruct((nh, N_KEYS, n), I32),
                   jax.ShapeDtypeStruct((nh, kk * kk, n), F32)],
        scratch_shapes=[pltpu.VMEM((kk, tt), F32), pltpu.VMEM((kk, tt), I32),
                        pltpu.VMEM((kk, tt), F32), pltpu.VMEM((kk, tt), I32),
                        pltpu.VMEM((kk, tt), F32), pltpu.VMEM((kk, tt), I32),
                        pltpu.VMEM((kk * kk, tt), F32)],
        compiler_params=_params(("parallel",)),
        name="peer_topk",
    )(q, k1, k2)


def _gather_rows8(table, idx):
    return jnp.take_along_axis(table, idx, axis=0, mode="promise_in_bounds")


def _peer_dense_kernel(xt_ref, x_ref, u_ref, vt_ref, inv1_ref, inv2_ref, p_ref, g_ref, b_ref,
                       o_ref, ft_ref, *, ag):
    a_blk = pl.program_id(1)
    kk = PEER_TOPK

    @pl.when(a_blk == 0)
    def _():
        ft_ref[...] = jnp.zeros_like(ft_ref)

    ht = _dot(u_ref[...], xt_ref[...])
    act = 0.5 * ht * (1.0 + lax.erf(ht * (1.0 / math.sqrt(2.0))))

    gts = []
    for aa in range(ag):
        a = a_blk * ag + aa

        def head_body(h, acc, a=a):
            i_row = inv1_ref[h, pl.ds(a, 1), :]
            prow = jnp.zeros((kk, acc.shape[1]), F32)
            for i in range(kk):
                prow = jnp.where(i_row == i, p_ref[h, i * kk:(i + 1) * kk, :], prow)
            lo, hi = prow[:SUBLANES], prow[SUBLANES:]
            pieces = []
            for r in range(N_KEYS // SUBLANES):
                j = inv2_ref[h, r * SUBLANES:(r + 1) * SUBLANES, :]
                j7 = j & (SUBLANES - 1)
                gv = jnp.where(j < SUBLANES, _gather_rows8(lo, j7), _gather_rows8(hi, j7))
                pieces.append(jnp.where(j < kk, gv, 0.0))
            return acc + jnp.concatenate(pieces, axis=0)

        gts.append(lax.fori_loop(0, PEER_HEADS, head_body, jnp.zeros((N_KEYS, ht.shape[1]), F32)))
    gt = jnp.concatenate(gts, axis=0) if ag > 1 else gts[0]
    zt = (gt * act).astype(BF16)
    ft_ref[...] += _dot(vt_ref[...], zt)

    @pl.when(a_blk == pl.num_programs(1) - 1)
    def _():
        y = DEEPNORM_ALPHA * x_ref[...] + ft_ref[...].T
        o_ref[...] = _layernorm(y, g_ref[...], b_ref[...])


def _peer_dense(x, xt, u, vt, inv1, inv2, p, g, b, tm, ag=2):
    n, d = x.shape
    ne = u.shape[0]
    te = ag * N_KEYS
    nh = PEER_HEADS
    kk = PEER_TOPK
    return pl.pallas_call(
        functools.partial(_peer_dense_kernel, ag=ag),
        grid=(n // tm, ne // te),
        in_specs=[pl.BlockSpec((d, tm), lambda i, a: (0, i)),
                  pl.BlockSpec((tm, d), lambda i, a: (i, 0)),
                  pl.BlockSpec((te, d), lambda i, a: (a, 0)),
                  pl.BlockSpec((d, te), lambda i, a: (0, a)),
                  pl.BlockSpec((nh, N_KEYS, tm), lambda i, a: (0, 0, i)),
                  pl.BlockSpec((nh, N_KEYS, tm), lambda i, a: (0, 0, i)),
                  pl.BlockSpec((nh, kk * kk, tm), lambda i, a: (0, 0, i)),
                  pl.BlockSpec((1, d), lambda i, a: (0, 0)),
                  pl.BlockSpec((1, d), lambda i, a: (0, 0))],
        out_specs=pl.BlockSpec((tm, d), lambda i, a: (i, 0)),
        out_shape=jax.ShapeDtypeStruct((n, d), F32),
        scratch_shapes=[pltpu.VMEM((d, tm), F32)],
        compiler_params=_params(("parallel", "arbitrary")),
        name="peer_dense",
    )(xt, x, u, vt, inv1, inv2, p, g.reshape(1, d), b.reshape(1, d))


def _transpose_kernel(x_ref, o_ref):
    o_ref[...] = x_ref[...].T.astype(o_ref.dtype)


def _transpose_bf16(x, tm):
    n, d = x.shape
    return pl.pallas_call(
        _transpose_kernel,
        grid=(n // tm,),
        in_specs=[pl.BlockSpec((tm, d), lambda i: (i, 0))],
        out_specs=pl.BlockSpec((d, tm), lambda i: (0, i)),
        out_shape=jax.ShapeDtypeStruct((d, n), BF16),
        compiler_params=_params(("parallel",)),
        name="transpose_bf16",
    )(x)


def _peer_layer(x, wq, k1, k2, u, vt, g, b, tm):
    n = x.shape[0]
    q = _matmul(x, wq, min(tm, 512), 512)
    inv1, inv2, p = _peer_topk(q, k1, k2, min(n, 2 * LANES))
    xt = _transpose_bf16(x, min(n, 512))
    return _peer_dense(x, xt, u, vt, inv1, inv2, p, g, b, tm)


def _mla_prep_kernel(x_ref, wkv_ref, wdq_ref, wq_ref, kvg_ref, qg_ref, cos_ref, sin_ref,
                     ckv_ref, ckvb_ref, kpe_ref, kpeb_ref, qn_ref, qpe_ref):
    xb = x_ref[...].astype(BF16)
    cos = cos_ref[...]
    sin = sin_ref[...]
    kv = _dot(xb, wkv_ref[...])
    ckv = _rmsnorm(kv[:, :KV_LORA], kvg_ref[...])
    ckv_ref[...] = ckv
    ckvb_ref[...] = ckv.astype(BF16)
    kr = kv[:, KV_LORA:KV_LORA + LANES]
    kr_sw = kv[:, KV_LORA + LANES:]
    kpe = kr * cos[:, :LANES] + kr_sw * sin[:, :LANES]
    kpe_ref[...] = kpe
    kpeb_ref[...] = kpe.astype(BF16)
    cq = _rmsnorm(_dot(xb, wdq_ref[...]), qg_ref[...]).astype(BF16)
    qa = _dot(cq, wq_ref[...])
    nn = qn_ref.shape[1]
    nr = qpe_ref.shape[1]
    qn_ref[...] = qa[:, :nn].astype(BF16)
    qpe_ref[...] = (qa[:, nn:nn + nr] * cos + qa[:, nn + nr:] * sin).astype(BF16)


def _mla_prep(x, wkv, wdq, wq, kvg, qg, cos, sin, tm):
    n, d = x.shape
    nn = MLA_HEADS * LANES
    nr = MLA_HEADS * QK_ROPE
    row = lambda i: (i, 0)
    fixed = lambda i: (0, 0)
    t = cos.shape[0]
    if t >= tm:
        rope_spec = pl.BlockSpec((tm, nr), lambda i: (i % (t // tm), 0))
    else:
        assert t == 1
        rope_spec = pl.BlockSpec((1, nr), fixed)
    return pl.pallas_call(
        _mla_prep_kernel,
        grid=(n // tm,),
        in_specs=[pl.BlockSpec((tm, d), row), pl.BlockSpec(wkv.shape, fixed), pl.BlockSpec(wdq.shape, fixed),
                  pl.BlockSpec(wq.shape, fixed), pl.BlockSpec((1, KV_LORA), fixed), pl.BlockSpec((1, Q_LORA), fixed),
                  rope_spec, rope_spec],
        out_specs=[pl.BlockSpec((tm, KV_LORA), row), pl.BlockSpec((tm, KV_LORA), row),
                   pl.BlockSpec((tm, LANES), row), pl.BlockSpec((tm, LANES), row),
                   pl.BlockSpec((tm, nn), row), pl.BlockSpec((tm, nr), row)],
        out_shape=[jax.ShapeDtypeStruct((n, KV_LORA), F32), jax.ShapeDtypeStruct((n, KV_LORA), BF16),
                   jax.ShapeDtypeStruct((n, LANES), F32), jax.ShapeDtypeStruct((n, LANES), BF16),
                   jax.ShapeDtypeStruct((n, nn), BF16), jax.ShapeDtypeStruct((n, nr), BF16)],
        compiler_params=_params(("parallel",)),
        name="mla_prep",
    )(x, wkv, wdq, wq, kvg.reshape(1, KV_LORA), qg.reshape(1, Q_LORA), cos, sin)


def _head_rope_lanes(qpe, h):
    lane = lax.broadcasted_iota(I32, (1, LANES), 1)
    return jnp.where((lane >> int(math.log2(QK_ROPE))) == (h & (LANES // QK_ROPE - 1)), qpe, jnp.zeros_like(qpe))


def _mla_prompt_kernel(qn_ref, qpe_ref, ckv_ref, kpe_ref, wuk_ref, o_ref, m_ref, l_ref, acc_ref, *, tq, tk):
    qi = pl.program_id(1)
    h = pl.program_id(2)
    ql = _dot(qn_ref[...], wuk_ref[...]).astype(BF16)
    qp = _head_rope_lanes(qpe_ref[...], h)
    m_ref[...] = jnp.full_like(m_ref, -jnp.inf)
    l_ref[...] = jnp.zeros_like(l_ref)
    acc_ref[...] = jnp.zeros_like(acc_ref)
    qpos = qi * tq + lax.broadcasted_iota(I32, (tq, 1), 0)

    def body(j, carry):
        off = pl.multiple_of(j * tk, tk)
        kc = ckv_ref[pl.ds(off, tk), :]
        kp = kpe_ref[pl.ds(off, tk), :]
        s = (_dot_nt(ql, kc) + _dot_nt(qp, kp)) * MLA_SCALE
        kpos = off + lax.broadcasted_iota(I32, (1, tk), 1)
        s = jnp.where(kpos <= qpos, s, -jnp.inf)
        m_old = m_ref[...]
        m_new = jnp.maximum(m_old, jnp.max(s, axis=-1, keepdims=True))
        corr = jnp.exp(m_old - m_new)
        p = jnp.exp(s - m_new)
        l_ref[...] = l_ref[...] * corr + jnp.sum(p, axis=-1, keepdims=True)
        acc_ref[...] = acc_ref[...] * corr + _dot(p.astype(BF16), kc)
        m_ref[...] = m_new
        return carry

    nkv = ((qi + 1) * tq + tk - 1) // tk
    lax.fori_loop(0, nkv, body, 0)
    o_ref[...] = (acc_ref[...] / l_ref[...]).astype(o_ref.dtype)


def _mla_prompt_attention(qn, qpe, ckvb, kpeb, wuk, batch, t, tq, tk):
    n = qn.shape[0]
    nq = t // tq
    grp = LANES // QK_ROPE
    return pl.pallas_call(
        functools.partial(_mla_prompt_kernel, tq=tq, tk=tk),
        grid=(batch, nq, MLA_HEADS),
        in_specs=[pl.BlockSpec((tq, LANES), lambda b, i, h: (b * nq + i, h)),
                  pl.BlockSpec((tq, LANES), lambda b, i, h: (b * nq + i, h // grp)),
                  pl.BlockSpec((t, KV_LORA), lambda b, i, h: (b, 0)),
                  pl.BlockSpec((t, LANES), lambda b, i, h: (b, 0)),
                  pl.BlockSpec((None, LANES, KV_LORA), lambda b, i, h: (h, 0, 0))],
        out_specs=pl.BlockSpec((tq, KV_LORA), lambda b, i, h: (b * nq + i, h)),
        out_shape=jax.ShapeDtypeStruct((n, MLA_HEADS * KV_LORA), BF16),
        scratch_shapes=[pltpu.VMEM((tq, 1), F32), pltpu.VMEM((tq, 1), F32), pltpu.VMEM((tq, KV_LORA), F32)],
        compiler_params=_params(("parallel", "parallel", "arbitrary")),
        name="mla_prompt_attention",
    )(qn, qpe, ckvb, kpeb, wuk)


def _qlat_kernel(qn_ref, wuk_ref, o_ref):
    o_ref[...] = _dot(qn_ref[...], wuk_ref[...])


def _mla_decode_kernel(pt_ref, ql_ref, qp_ref, cn_ref, kn_ref, *refs, g):
    ck_refs = refs[:g]
    kp_refs = refs[g:2 * g]
    o_ref, m_ref, l_ref, acc_ref, kbuf_ref, pbuf_ref = refs[2 * g:]
    s_id = pl.program_id(1)

    @pl.when(s_id == 0)
    def _():
        m_ref[...] = jnp.full_like(m_ref, -1e30)
        l_ref[...] = jnp.zeros_like(l_ref)
        acc_ref[...] = jnp.zeros_like(acc_ref)

    for p in range(g):
        kbuf_ref[p * PAGE_SIZE:(p + 1) * PAGE_SIZE, :] = ck_refs[p][...].astype(BF16)
        pbuf_ref[p * PAGE_SIZE:(p + 1) * PAGE_SIZE, :] = kp_refs[p][...].astype(BF16)
    ql = ql_ref[...]
    qp = qp_ref[...]
    kc = kbuf_ref[...]
    s = (_dot_nt(ql.astype(BF16), kc) + _dot_nt(qp, pbuf_ref[...])) * MLA_SCALE
    m_old = m_ref[...]
    m_new = jnp.maximum(m_old, jnp.max(s, axis=-1, keepdims=True))
    corr = jnp.exp(m_old - m_new)
    pr = jnp.exp(s - m_new)
    l_new = l_ref[...] * corr + jnp.sum(pr, axis=-1, keepdims=True)
    acc_new = acc_ref[...] * corr + _dot(pr.astype(BF16), kc)
    m_ref[...] = m_new
    l_ref[...] = l_new
    acc_ref[...] = acc_new

    @pl.when(s_id == pl.num_programs(1) - 1)
    def _():
        cn = cn_ref[...]
        sn = (jnp.sum(ql * cn, axis=-1, keepdims=True)
              + jnp.sum(qp.astype(F32) * kn_ref[...], axis=-1, keepdims=True)) * MLA_SCALE
        m_fin = jnp.maximum(m_new, sn)
        c2 = jnp.exp(m_new - m_fin)
        pn = jnp.exp(sn - m_fin)
        l_fin = l_new * c2 + pn
        o_ref[...] = ((acc_new * c2 + pn * cn) / l_fin).astype(o_ref.dtype)


def _mla_decode_attention(qlat, qpe, ckv_new, kpe_new, cache_ckv, cache_kpe, page_table, g=16):
    bsz, nh, _ = qlat.shape
    n_pages = page_table.shape[1]
    steps = n_pages // g
    per_b = lambda b, s, pt: (b, 0, 0)

    def page(p):
        return lambda b, s, pt: (pt[b, s * g + p], 0, 0)

    specs = [pl.BlockSpec((None, nh, KV_LORA), per_b), pl.BlockSpec((None, nh, QK_ROPE), per_b),
             pl.BlockSpec((None, 1, KV_LORA), per_b), pl.BlockSpec((None, 1, QK_ROPE), per_b)]
    specs += [pl.BlockSpec((None, PAGE_SIZE, KV_LORA), page(p)) for p in range(g)]
    specs += [pl.BlockSpec((None, PAGE_SIZE, QK_ROPE), page(p)) for p in range(g)]
    return pl.pallas_call(
        functools.partial(_mla_decode_kernel, g=g),
        grid_spec=pltpu.PrefetchScalarGridSpec(
            num_scalar_prefetch=1,
            grid=(bsz, steps),
            in_specs=specs,
            out_specs=pl.BlockSpec((None, nh, KV_LORA), per_b),
            scratch_shapes=[pltpu.VMEM((nh, 1), F32), pltpu.VMEM((nh, 1), F32), pltpu.VMEM((nh, KV_LORA), F32),
                            pltpu.VMEM((g * PAGE_SIZE, KV_LORA), BF16), pltpu.VMEM((g * PAGE_SIZE, QK_ROPE), BF16)]),
        out_shape=jax.ShapeDtypeStruct((bsz, nh, KV_LORA), BF16),
        compiler_params=_params(("parallel", "arbitrary")),
        name="mla_decode_attention",
    )(page_table, qlat, qpe, ckv_new, kpe_new, *([cache_ckv] * g), *([cache_kpe] * g))


def _rope_tables(pos):
    half = QK_ROPE // 2
    inv = ROPE_THETA ** (-jnp.arange(half, dtype=F32) / half)
    ang = pos.astype(F32)[:, None] * inv[None, :]
    cos = jnp.cos(ang)
    sin = jnp.sin(ang)
    cos_f = jnp.concatenate([cos, cos], axis=-1)
    sin_f = jnp.concatenate([-sin, sin], axis=-1)
    return jnp.tile(cos_f, (1, MLA_HEADS)), jnp.tile(sin_f, (1, MLA_HEADS))


def _swap_halves(w):
    half = w.shape[-1] // 2
    return jnp.concatenate([w[..., half:], w[..., :half]], axis=-1)


def kernel(x_prompt, x_sample, state_hgrn, cache_ckv, cache_kpe, page_table, hgrn_w_in, hgrn_lb_logits, hgrn_norm_g, hgrn_w_out, mla_w_dkv, mla_kv_norm_g, mla_w_ukv, mla_w_dq, mla_q_norm_g, mla_w_uq, mla_w_o, peer_w_q, peer_k1, peer_k2, peer_u, peer_v, ln_g, ln_b):
    bp, tp, d = x_prompt.shape
    bs, ts, _ = x_sample.shape
    nh = MLA_HEADS
    past_len = page_table.shape[1] * cache_ckv.shape[1]

    w_in = hgrn_w_in.astype(BF16)
    w_out = hgrn_w_out.astype(BF16)
    wq_peer = peer_w_q.astype(BF16)
    k1 = peer_k1.astype(BF16)
    k2 = peer_k2.astype(BF16)
    u_b = peer_u.astype(BF16)
    vt_b = jnp.swapaxes(peer_v, 1, 2).astype(BF16)
    w_ckv = mla_w_dkv[:, :KV_LORA]
    w_kr = mla_w_dkv[:, KV_LORA:]
    rep = LANES // QK_ROPE
    wkv = jnp.concatenate([w_ckv, jnp.tile(w_kr, (1, rep)), jnp.tile(_swap_halves(w_kr), (1, rep))], axis=1).astype(BF16)
    w_uk = mla_w_ukv[..., :QK_NOPE]
    w_uv = mla_w_ukv[..., QK_NOPE:]
    wuk_p = jnp.pad(jnp.transpose(w_uk, (1, 2, 0)), ((0, 0), (0, LANES - QK_NOPE), (0, 0))).astype(BF16)
    eye = jnp.eye(nh, dtype=F32)
    wuv_bd = (jnp.transpose(w_uv, (1, 0, 2))[:, :, None, :] * eye[:, None, :, None]).reshape(nh * KV_LORA, nh * V_DIM).astype(BF16)

    def mla_q_weights(j):
        wu = mla_w_uq[j].reshape(Q_LORA, nh, QK_NOPE + QK_ROPE)
        nope = jnp.pad(wu[..., :QK_NOPE], ((0, 0), (0, 0), (0, LANES - QK_NOPE))).reshape(Q_LORA, nh * LANES)
        rope = wu[..., QK_NOPE:]
        return jnp.concatenate([nope, rope.reshape(Q_LORA, nh * QK_ROPE),
                                _swap_halves(rope).reshape(Q_LORA, nh * QK_ROPE)], axis=1).astype(BF16)

    def trunk(x, batch, t, pos, s0_of, attend):
        n = batch * t
        tm = min(n, 512)
        states = []
        ckv = kpe = None
        ckvb = kpeb = None
        for l in range(DEPTH):
            if l < N_A_LAYERS:
                proj = _matmul(x, w_in[l], tm, 512)
                og, s_new = _hgrn_mixer(proj, batch, t, hgrn_lb_logits, hgrn_norm_g[l], s0_of(l), l)
                states.append(s_new)
                x = _matmul_residual_ln(og, w_out[l], x, ln_g[l, 0], ln_b[l, 0], min(tm, 256))
            else:
                j = l - N_A_LAYERS
                cos, sin = _rope_tables(pos)
                outs = _mla_prep(x, wkv, mla_w_dq[j].astype(BF16), mla_q_weights(j), mla_kv_norm_g,
                                 mla_q_norm_g[j], cos, sin, min(tm, 256))
                ckv_j, ckvb_j, kpe_j, kpeb_j, qn, qpe = outs
                if j == 0:
                    ckv, ckvb, kpe, kpeb = ckv_j, ckvb_j, kpe_j, kpeb_j
                o_lat = attend(qn, qpe, ckv, kpe, ckvb, kpeb)
                x = _matmul_residual_ln(o_lat, mla_w_o[j].astype(BF16), x, ln_g[l, 0], ln_b[l, 0],
                                        min(tm, 256), w0=wuv_bd)
            x = _peer_layer(x, wq_peer[l], k1[l], k2[l], u_b[l], vt_b[l], ln_g[l, 1], ln_b[l, 1], tm)
        return x, jnp.stack(states), ckv, kpe[:, :QK_ROPE]

    def attend_prompt(qn, qpe, ckv, kpe, ckvb, kpeb):
        return _mla_prompt_attention(qn, qpe, ckvb, kpeb, wuk_p, bp, tp, 512, 512)

    def attend_sample(qn, qpe, ckv, kpe, ckvb, kpeb):
        n = qn.shape[0]
        qlat = pl.pallas_call(
            _qlat_kernel,
            grid=(nh,),
            in_specs=[pl.BlockSpec((n, LANES), lambda h: (0, h)),
                      pl.BlockSpec((None, LANES, KV_LORA), lambda h: (h, 0, 0))],
            out_specs=pl.BlockSpec((None, n, KV_LORA), lambda h: (h, 0, 0)),
            out_shape=jax.ShapeDtypeStruct((nh, n, KV_LORA), F32),
            compiler_params=_params(("parallel",)),
            name="mla_q_latent",
        )(qn, wuk_p)
        qlat = jnp.transpose(qlat, (1, 0, 2))
        o = _mla_decode_attention(qlat, qpe.reshape(n, nh, QK_ROPE), ckv.reshape(n, 1, KV_LORA),
                                  kpe[:, :QK_ROPE].reshape(n, 1, QK_ROPE), cache_ckv, cache_kpe, page_table)
        return o.reshape(n, nh * KV_LORA)

    y_p, st_p, ckv_p, kpe_p = trunk(x_prompt.reshape(bp * tp, d), bp, tp, jnp.arange(tp),
                                    lambda l: None, attend_prompt)
    y_s, st_s, ckv_s, kpe_s = trunk(x_sample.reshape(bs * ts, d), bs, ts, past_len + jnp.arange(ts),
                                    lambda l: state_hgrn[l], attend_sample)
    return (y_p.reshape(bp, tp, d), y_s.reshape(bs, ts, d), st_p, st_s,
            ckv_p.reshape(bp, tp, KV_LORA), kpe_p.reshape(bp, tp, QK_ROPE),
            ckv_s.reshape(bs, ts, KV_LORA), kpe_s.reshape(bs, ts, QK_ROPE))
```

```python
import functools
import math

import numpy as np
import jax
import jax.numpy as jnp
from jax import lax
from jax.experimental import pallas as pl
from jax.experimental.pallas import tpu as pltpu

F32 = jnp.float32
BF16 = jnp.bfloat16
I32 = jnp.int32
I16 = jnp.int16

D_MODEL = 1024
DEPTH = 2
N_A_LAYERS = DEPTH // 2
HGRN_HEADS = 8
HGRN_DK = 128
HGRN_DV = 128
MLA_HEADS = 16
QK_NOPE = 64
QK_ROPE = 32
V_DIM = 64
KV_LORA = 256
Q_LORA = 384
ROPE_THETA = 10000.0
MLA_SCALE = (QK_NOPE + QK_ROPE) ** -0.5
PAGE_SIZE = 128
PEER_HEADS = 8
PEER_KEY_DIM = 256
N_KEYS = 128
PEER_TOPK = 16
DEEPNORM_ALPHA = (2.0 * DEPTH) ** 0.25
LN_EPS = 1e-5
RMS_EPS = 1e-6

LANES = 128
SUBLANES = 8
PACKED_ROWS = 16
VMEM_LIMIT_BYTES = 56 * 1024 * 1024


def _params(sem):
    return pltpu.CompilerParams(dimension_semantics=sem, vmem_limit_bytes=VMEM_LIMIT_BYTES)


def _dot(a, b):
    return jnp.dot(a, b, preferred_element_type=F32)


def _dot_nt(a, b):
    return lax.dot_general(a, b, (((1,), (1,)), ((), ())), preferred_element_type=F32)


def _dot_tn(a, b):
    return lax.dot_general(a, b, (((0,), (0,)), ((), ())), preferred_element_type=F32)


def _layernorm(x, g, b):
    mu = jnp.mean(x, axis=-1, keepdims=True)
    xc = x - mu
    var = jnp.mean(xc * xc, axis=-1, keepdims=True)
    return xc * lax.rsqrt(var + LN_EPS) * g + b


def _rmsnorm(x, g):
    return x * lax.rsqrt(jnp.mean(x * x, axis=-1, keepdims=True) + RMS_EPS) * g


def _mm_kernel(x_ref, w_ref, o_ref):
    o_ref[...] = _dot(x_ref[...].astype(BF16), w_ref[...]).astype(o_ref.dtype)


def _matmul(x, w, tm, tn, out_dtype=F32):
    m, k = x.shape
    n = w.shape[1]
    return pl.pallas_call(
        _mm_kernel,
        grid=(m // tm, n // tn),
        in_specs=[pl.BlockSpec((tm, k), lambda i, j: (i, 0)),
                  pl.BlockSpec((k, tn), lambda i, j: (0, j))],
        out_specs=pl.BlockSpec((tm, tn), lambda i, j: (i, j)),
        out_shape=jax.ShapeDtypeStruct((m, n), out_dtype),
        compiler_params=_params(("parallel", "parallel")),
        name="matmul",
    )(x, w)


def _mm_ln_kernel(*refs, two_stage):
    if two_stage:
        h_ref, w0_ref, w_ref, x_ref, g_ref, b_ref, o_ref = refs
        h = _dot(h_ref[...].astype(BF16), w0_ref[...]).astype(BF16)
    else:
        h_ref, w_ref, x_ref, g_ref, b_ref, o_ref = refs
        h = h_ref[...].astype(BF16)
    y = DEEPNORM_ALPHA * x_ref[...] + _dot(h, w_ref[...])
    o_ref[...] = _layernorm(y, g_ref[...], b_ref[...])


def _matmul_residual_ln(h, w, x, g, b, tm, w0=None):
    m, k = h.shape
    d = w.shape[1]
    row = lambda i: (i, 0)
    fixed = lambda i: (0, 0)
    ins = [h] + ([w0] if w0 is not None else []) + [w, x, g.reshape(1, d), b.reshape(1, d)]
    specs = [pl.BlockSpec((tm, k), row)]
    if w0 is not None:
        specs.append(pl.BlockSpec(w0.shape, fixed))
    specs += [pl.BlockSpec(w.shape, fixed), pl.BlockSpec((tm, d), row),
              pl.BlockSpec((1, d), fixed), pl.BlockSpec((1, d), fixed)]
    return pl.pallas_call(
        functools.partial(_mm_ln_kernel, two_stage=w0 is not None),
        grid=(m // tm,),
        in_specs=specs,
        out_specs=pl.BlockSpec((tm, d), row),
        out_shape=jax.ShapeDtypeStruct((m, d), F32),
        compiler_params=_params(("parallel",)),
        name="matmul_residual_ln",
    )(*ins)


def _hgrn_level_matrix(c):
    t = np.arange(c)[:, None]
    s = np.arange(c)[None, :]
    mats = [(s <= t)]
    m = c // 2
    while m >= 1:
        bd = (t // (2 * m)) * (2 * m) + m - 1
        mats.append(s <= bd)
        m //= 2
    return np.concatenate(mats, axis=0).astype(np.float32)


def _hgrn_head(q, fz, v, lbl, ng, lvl, st, *, c, n_real, layer):
    dk = q.shape[-1]
    e = jnp.exp(lbl - jnp.max(lbl, axis=0, keepdims=True))
    sm = e / jnp.sum(e, axis=0, keepdims=True)
    lb = jnp.sum(sm[:layer + 1], axis=0, keepdims=True)

    logf = jnp.log(lb + (1.0 - lb) * jax.nn.sigmoid(fz))
    k = (1.0 - lb) * jax.nn.sigmoid(-fz)
    row_id = lax.broadcasted_iota(I32, (c, 1), 0)
    if n_real != c:
        live = row_id < n_real
        logf = jnp.where(live, logf, 0.0)
        k = jnp.where(live, k, 0.0)

    hi = logf.astype(BF16)
    r1 = logf - hi.astype(F32)
    mid = r1.astype(BF16)
    lo = (r1 - mid.astype(F32)).astype(BF16)
    parts = _dot(lvl, jnp.concatenate([hi, mid, lo], axis=1))
    sums = parts[:, :dk] + (parts[:, dk:2 * dk] + parts[:, 2 * dk:])
    b = sums[:c]

    kb = k.astype(BF16)
    vb = v.astype(BF16)
    col_id = lax.broadcasted_iota(I32, (1, c), 1)
    att = jnp.where(row_id == col_id, _dot_nt(q.astype(BF16), kb), 0.0)
    m = c // 2
    level = 1
    while m >= 1:
        r = sums[level * c:(level + 1) * c]
        second = (row_id & (2 * m - 1)) >= m
        ql = q * jnp.exp(jnp.where(second, b - r, -jnp.inf))
        kl = k * jnp.exp(jnp.where(second, -jnp.inf, r - b))
        a = _dot_nt(ql.astype(BF16), kl.astype(BF16))
        if 2 * m == c:
            att = att + a
        else:
            sh = int(math.log2(2 * m))
            att = att + jnp.where((row_id >> sh) == (col_id >> sh), a, 0.0)
        m //= 2
        level += 1

    o = _dot_nt((q * jnp.exp(b)).astype(BF16), st.astype(BF16)) + _dot(att.astype(BF16), vb)
    b_last = b[c - 1:c, :]
    st_new = st * jnp.exp(b_last) + _dot_tn(vb, (k * jnp.exp(b_last - b)).astype(BF16))
    return _rmsnorm(o, ng), st_new


def _hgrn_kernel(*refs, c, n_real, layer, has_s0, hg):
    if has_s0:
        (q_ref, f_ref, i_ref, g_ref, lbl_ref, ng_ref, lvl_ref, s0_ref, o_ref, s_ref, st_ref) = refs
    else:
        (q_ref, f_ref, i_ref, g_ref, lbl_ref, ng_ref, lvl_ref, o_ref, s_ref, st_ref) = refs
    hb = pl.program_id(1)
    ci = pl.program_id(2)
    dk = HGRN_DK
    rows = q_ref.shape[0]

    @pl.when(ci == 0)
    def _():
        for hh in range(hg):
            st_ref[hh] = s0_ref[hh].T if has_s0 else jnp.zeros((HGRN_DV, HGRN_DK), F32)

    def rows_of(ref, cols):
        x = ref[:, cols]
        return jnp.broadcast_to(x, (c, dk)) if rows != c else x

    for hh in range(hg):
        cols = slice(hh * dk, (hh + 1) * dk)
        ng = ng_ref[pl.ds(hb * hg + hh, 1), :]
        o, st_new = _hgrn_head(rows_of(q_ref, cols), rows_of(f_ref, cols), rows_of(i_ref, cols),
                               lbl_ref[:, cols], ng, lvl_ref[...], st_ref[hh], c=c, n_real=n_real, layer=layer)
        st_ref[hh] = st_new
        g = g_ref[:, cols]
        o_ref[:, cols] = o[:rows] * (g * jax.nn.sigmoid(g))

        @pl.when(ci == pl.num_programs(2) - 1)
        def _(hh=hh, st_new=st_new):
            s_ref[hh] = st_new.T


def _hgrn_mixer(proj, batch, t, lb_logits, norm_g, s0, layer):
    n = proj.shape[0]
    hd = HGRN_HEADS * HGRN_DK
    nh = HGRN_HEADS
    if t >= LANES:
        c, rows, n_real, hg = LANES, LANES, LANES, 1
        w = hg * HGRN_DK
        arr = proj
        col = lambda off: pl.BlockSpec((rows, w), lambda b, h, ci: (b * (t // c) + ci, off * (nh // hg) + h))
        o_spec = pl.BlockSpec((rows, w), lambda b, h, ci: (b * (t // c) + ci, h))
        o_shape = jax.ShapeDtypeStruct((n, nh * HGRN_DV), F32)
        nchunk = t // c
    else:
        assert t == 1
        c, rows, n_real, hg = 2 * SUBLANES, 1, 1, nh
        w = hg * HGRN_DK
        arr = proj.reshape(n, 1, 4 * hd)
        col = lambda off: pl.BlockSpec((None, 1, w), lambda b, h, ci: (b, 0, off * (nh // hg) + h))
        o_spec = pl.BlockSpec((None, 1, w), lambda b, h, ci: (b, 0, h))
        o_shape = jax.ShapeDtypeStruct((n, 1, nh * HGRN_DV), F32)
        nchunk = 1
    lvl = jnp.asarray(_hgrn_level_matrix(c), BF16)
    nl = lb_logits.shape[0]
    state_spec = pl.BlockSpec((None, hg, HGRN_DK, HGRN_DV), lambda b, h, ci: (b, h, 0, 0))
    ins = [arr, arr, arr, arr, lb_logits, norm_g, lvl]
    specs = [col(0), col(1), col(2), col(3),
             pl.BlockSpec((nl, w), lambda b, h, ci: (0, h)),
             pl.BlockSpec((nh, HGRN_DV), lambda b, h, ci: (0, 0)),
             pl.BlockSpec(lvl.shape, lambda b, h, ci: (0, 0))]
    if s0 is not None:
        ins.append(s0)
        specs.append(state_spec)
    o, s = pl.pallas_call(
        functools.partial(_hgrn_kernel, c=c, n_real=n_real, layer=layer, has_s0=s0 is not None, hg=hg),
        grid=(batch, nh // hg, nchunk),
        in_specs=specs,
        out_specs=[o_spec, state_spec],
        out_shape=[o_shape, jax.ShapeDtypeStruct((batch, nh, HGRN_DK, HGRN_DV), F32)],
        scratch_shapes=[pltpu.VMEM((hg, HGRN_DV, HGRN_DK), F32)],
        compiler_params=_params(("parallel", "parallel", "arbitrary")),
        name="hgrn2_chunk",
    )(*ins)
    return o.reshape(n, nh * HGRN_DV), s


def _top16(s, val_ref):
    n = s.shape[0]
    rid = lax.broadcasted_iota(I32, s.shape, 0)
    rank = jnp.full(s.shape, PEER_TOPK, I32)
    for kk in range(PEER_TOPK):
        mx = jnp.max(s, axis=0, keepdims=True)
        ix = jnp.min(jnp.where(s == mx, rid, n), axis=0, keepdims=True)
        hit = rid == ix
        val_ref[kk:kk + 1, :] = mx
        rank = jnp.where(hit, kk, rank)
        s = jnp.where(hit, -jnp.inf, s)
    return rank


_PAIR_COUNT = [PEER_TOPK // (i + 1) for i in range(PEER_TOPK)]
_PAIR_START = [sum(_PAIR_COUNT[:i]) for i in range(PEER_TOPK)]
_N_PAIRS = sum(_PAIR_COUNT)
_N_PAIR_ROWS = -(-_N_PAIRS // SUBLANES) * SUBLANES


def _peer_topk_kernel(q_ref, k1_ref, k2_ref, inv1_ref, inv2_ref, p_ref, v1_ref, v2_ref, sv_ref, cand_ref, pt_ref):
    half = PEER_KEY_DIM // 2
    kk = PEER_TOPK
    for h in range(PEER_HEADS):
        q = q_ref[:, h * PEER_KEY_DIM:(h + 1) * PEER_KEY_DIM].astype(BF16)
        s1 = _dot_nt(k1_ref[h], q[:, :half])
        s2 = _dot_nt(k2_ref[h], q[:, half:])
        inv1_ref[h] = _top16(s1, v1_ref)
        inv2_ref[h] = _top16(s2, v2_ref).astype(I16)
        cand_ref[...] = jnp.full(cand_ref.shape, -jnp.inf, F32)
        for i in range(kk):
            cand_ref[_PAIR_START[i]:_PAIR_START[i] + _PAIR_COUNT[i], :] = v1_ref[i:i + 1, :] + v2_ref[0:_PAIR_COUNT[i], :]
        rank = _top16(cand_ref[...], sv_ref)
        sv = sv_ref[...]
        e = jnp.exp(sv - jnp.max(sv, axis=0, keepdims=True))
        gate = e / jnp.sum(e, axis=0, keepdims=True)
        p = jnp.zeros(rank.shape, F32)
        for j in range(kk):
            p = jnp.where(rank == j, gate[j:j + 1, :], p)
        pt_ref[...] = jnp.zeros(pt_ref.shape, F32)
        for i in range(kk):
            pt_ref[i * kk:i * kk + _PAIR_COUNT[i], :] = p[_PAIR_START[i]:_PAIR_START[i] + _PAIR_COUNT[i], :]
        p_ref[h] = pt_ref[...].astype(BF16)


def _peer_topk(q, k1, k2, tt):
    n = q.shape[0]
    nh = PEER_HEADS
    kk = PEER_TOPK
    return pl.pallas_call(
        _peer_topk_kernel,
        grid=(n // tt,),
        in_specs=[pl.BlockSpec((tt, nh * PEER_KEY_DIM), lambda i: (i, 0)),
                  pl.BlockSpec(k1.shape, lambda i: (0, 0, 0)),
                  pl.BlockSpec(k2.shape, lambda i: (0, 0, 0))],
        out_specs=[pl.BlockSpec((nh, N_KEYS, tt), lambda i: (0, 0, i)),
                   pl.BlockSpec((nh, N_KEYS, tt), lambda i: (0, 0, i)),
                   pl.BlockSpec((nh, kk * kk, tt), lambda i: (0, 0, i))],
        out_shape=[jax.ShapeDtypeStruct((nh, N_KEYS, n), I32),
                   jax.ShapeDtypeStruct((nh, N_KEYS, n), I16),
                   jax.ShapeDtypeStruct((nh, kk * kk, n), BF16)],
        scratch_shapes=[pltpu.VMEM((kk, tt), F32), pltpu.VMEM((kk, tt), F32), pltpu.VMEM((kk, tt), F32),
                        pltpu.VMEM((_N_PAIR_ROWS, tt), F32), pltpu.VMEM((kk * kk, tt), F32)],
        compiler_params=_params(("parallel",)),
        name="peer_topk",
    )(q, k1, k2)


_GATHER_ROWS = lax.GatherDimensionNumbers(offset_dims=(), collapsed_slice_dims=(0,), start_index_map=(0,),
                                          operand_batching_dims=(1,), start_indices_batching_dims=(1,))


def _gather_rows16(table, idx):
    return lax.gather(table, idx[..., None], _GATHER_ROWS, (1, 1), mode=lax.GatherScatterMode.PROMISE_IN_BOUNDS)


def _peer_dense_kernel(xt_ref, x_ref, u_ref, vt_ref, inv1_ref, inv2_ref, p_ref, g_ref, b_ref,
                       o_ref, ft_ref, prow_ref, zt_ref, act_ref, *, ag):
    a_blk = pl.program_id(1)
    kk = PEER_TOPK
    tm = xt_ref.shape[1]

    @pl.when(a_blk == 0)
    def _():
        ft_ref[...] = jnp.zeros_like(ft_ref)

    ht = _dot(u_ref[...], xt_ref[...])
    act_ref[...] = (0.5 * ht * (1.0 + lax.erf(ht * (1.0 / math.sqrt(2.0))))).astype(BF16)

    for aa in range(ag):
        a = a_blk * ag + aa
        for h in range(PEER_HEADS):
            i_row = jnp.broadcast_to(inv1_ref[h, pl.ds(a, 1), :].astype(I16), (kk, tm))
            rows = [p_ref[h, i * kk:(i + 1) * kk, :] for i in range(kk)]
            bit = 1
            while len(rows) > 1:
                odd = (i_row & bit) != 0
                rows = [jnp.where(odd, rows[2 * q + 1], rows[2 * q]) for q in range(len(rows) // 2)]
                bit *= 2
            prow_ref[aa, h] = jnp.where(i_row < kk, rows[0], jnp.zeros_like(rows[0]))

    for r in range(N_KEYS // PACKED_ROWS):
        accs = [jnp.zeros((PACKED_ROWS, tm), BF16) for _ in range(ag)]
        for h in range(PEER_HEADS):
            j = inv2_ref[h, r * PACKED_ROWS:(r + 1) * PACKED_ROWS, :]
            keep = jnp.where(j < kk, jnp.ones((), BF16), jnp.zeros((), BF16))
            jc = j & (kk - 1)
            for aa in range(ag):
                accs[aa] = accs[aa] + _gather_rows16(prow_ref[aa, h], jc) * keep
        for aa in range(ag):
            lo = aa * N_KEYS + r * PACKED_ROWS
            zt_ref[lo:lo + PACKED_ROWS, :] = accs[aa] * act_ref[lo:lo + PACKED_ROWS, :]

    ft_ref[...] += _dot(vt_ref[...], zt_ref[...])

    @pl.when(a_blk == pl.num_programs(1) - 1)
    def _():
        y = DEEPNORM_ALPHA * x_ref[...] + ft_ref[...].T
        o_ref[...] = _layernorm(y, g_ref[...], b_ref[...])


def _peer_dense(x, xt, u, vt, inv1, inv2, p, g, b, tm, ag=8):
    n, d = x.shape
    ne = u.shape[0]
    te = ag * N_KEYS
    nh = PEER_HEADS
    kk = PEER_TOPK
    return pl.pallas_call(
        functools.partial(_peer_dense_kernel, ag=ag),
        grid=(n // tm, ne // te),
        in_specs=[pl.BlockSpec((d, tm), lambda i, a: (0, i)),
                  pl.BlockSpec((tm, d), lambda i, a: (i, 0)),
                  pl.BlockSpec((te, d), lambda i, a: (a, 0)),
                  pl.BlockSpec((d, te), lambda i, a: (0, a)),
                  pl.BlockSpec((nh, N_KEYS, tm), lambda i, a: (0, 0, i)),
                  pl.BlockSpec((nh, N_KEYS, tm), lambda i, a: (0, 0, i)),
                  pl.BlockSpec((nh, kk * kk, tm), lambda i, a: (0, 0, i)),
                  pl.BlockSpec((1, d), lambda i, a: (0, 0)),
                  pl.BlockSpec((1, d), lambda i, a: (0, 0))],
        out_specs=pl.BlockSpec((tm, d), lambda i, a: (i, 0)),
        out_shape=jax.ShapeDtypeStruct((n, d), F32),
        scratch_shapes=[pltpu.VMEM((d, tm), F32), pltpu.VMEM((ag, nh, kk, tm), BF16),
                        pltpu.VMEM((te, tm), BF16), pltpu.VMEM((te, tm), BF16)],
        compiler_params=_params(("parallel", "arbitrary")),
        name="peer_dense",
    )(xt, x, u, vt, inv1, inv2, p, g.reshape(1, d), b.reshape(1, d))


def _transpose_kernel(x_ref, o_ref):
    o_ref[...] = x_ref[...].T.astype(o_ref.dtype)


def _transpose_bf16(x, tm):
    n, d = x.shape
    return pl.pallas_call(
        _transpose_kernel,
        grid=(n // tm,),
        in_specs=[pl.BlockSpec((tm, d), lambda i: (i, 0))],
        out_specs=pl.BlockSpec((d, tm), lambda i: (0, i)),
        out_shape=jax.ShapeDtypeStruct((d, n), BF16),
        compiler_params=_params(("parallel",)),
        name="transpose_bf16",
    )(x)


def _peer_layer(x, wq, k1, k2, u, vt, g, b, tm):
    n = x.shape[0]
    q = _matmul(x, wq, min(tm, 512), 512)
    inv1, inv2, p = _peer_topk(q, k1, k2, min(n, 2 * LANES))
    xt = _transpose_bf16(x, min(n, 512))
    return _peer_dense(x, xt, u, vt, inv1, inv2, p, g, b, tm)


def _mla_prep_kernel(x_ref, wkv_ref, wdq_ref, wq_ref, kvg_ref, qg_ref, cos_ref, sin_ref,
                     ckv_ref, ckvb_ref, kpe_ref, kpeb_ref, qn_ref, qpe_ref):
    xb = x_ref[...].astype(BF16)
    cos = cos_ref[...]
    sin = sin_ref[...]
    kv = _dot(xb, wkv_ref[...])
    ckv = _rmsnorm(kv[:, :KV_LORA], kvg_ref[...])
    ckv_ref[...] = ckv
    ckvb_ref[...] = ckv.astype(BF16)
    kr = kv[:, KV_LORA:KV_LORA + LANES]
    kr_sw = kv[:, KV_LORA + LANES:]
    kpe = kr * cos[:, :LANES] + kr_sw * sin[:, :LANES]
    kpe_ref[...] = kpe
    kpeb_ref[...] = kpe.astype(BF16)
    cq = _rmsnorm(_dot(xb, wdq_ref[...]), qg_ref[...]).astype(BF16)
    qa = _dot(cq, wq_ref[...])
    nn = qn_ref.shape[1]
    nr = qpe_ref.shape[1]
    qn_ref[...] = qa[:, :nn].astype(BF16)
    qpe_ref[...] = (qa[:, nn:nn + nr] * cos + qa[:, nn + nr:] * sin).astype(BF16)


def _mla_prep(x, wkv, wdq, wq, kvg, qg, cos, sin, tm):
    n, d = x.shape
    nn = MLA_HEADS * LANES
    nr = MLA_HEADS * QK_ROPE
    row = lambda i: (i, 0)
    fixed = lambda i: (0, 0)
    t = cos.shape[0]
    if t >= tm:
        rope_spec = pl.BlockSpec((tm, nr), lambda i: (i % (t // tm), 0))
    else:
        assert t == 1
        rope_spec = pl.BlockSpec((1, nr), fixed)
    return pl.pallas_call(
        _mla_prep_kernel,
        grid=(n // tm,),
        in_specs=[pl.BlockSpec((tm, d), row), pl.BlockSpec(wkv.shape, fixed), pl.BlockSpec(wdq.shape, fixed),
                  pl.BlockSpec(wq.shape, fixed), pl.BlockSpec((1, KV_LORA), fixed), pl.BlockSpec((1, Q_LORA), fixed),
                  rope_spec, rope_spec],
        out_specs=[pl.BlockSpec((tm, KV_LORA), row), pl.BlockSpec((tm, KV_LORA), row),
                   pl.BlockSpec((tm, LANES), row), pl.BlockSpec((tm, LANES), row),
                   pl.BlockSpec((tm, nn), row), pl.BlockSpec((tm, nr), row)],
        out_shape=[jax.ShapeDtypeStruct((n, KV_LORA), F32), jax.ShapeDtypeStruct((n, KV_LORA), BF16),
                   jax.ShapeDtypeStruct((n, LANES), F32), jax.ShapeDtypeStruct((n, LANES), BF16),
                   jax.ShapeDtypeStruct((n, nn), BF16), jax.ShapeDtypeStruct((n, nr), BF16)],
        compiler_params=_params(("parallel",)),
        name="mla_prep",
    )(x, wkv, wdq, wq, kvg.reshape(1, KV_LORA), qg.reshape(1, Q_LORA), cos, sin)


def _head_rope_lanes(qpe, h):
    lane = lax.broadcasted_iota(I32, (1, LANES), 1)
    return jnp.where((lane >> int(math.log2(QK_ROPE))) == (h & (LANES // QK_ROPE - 1)), qpe, jnp.zeros_like(qpe))


def _mla_prompt_kernel(qn_ref, qpe_ref, ckv_ref, kpe_ref, wuk_ref, o_ref, m_ref, l_ref, acc_ref, *, tq, tk):
    qi = pl.program_id(1)
    h = pl.program_id(2)
    ql = _dot(qn_ref[...], wuk_ref[...]).astype(BF16)
    qp = _head_rope_lanes(qpe_ref[...], h)
    m_ref[...] = jnp.full_like(m_ref, -jnp.inf)
    l_ref[...] = jnp.zeros_like(l_ref)
    acc_ref[...] = jnp.zeros_like(acc_ref)
    qpos = qi * tq + lax.broadcasted_iota(I32, (tq, 1), 0)

    def step(j, masked):
        off = pl.multiple_of(j * tk, tk)
        kc = ckv_ref[pl.ds(off, tk), :]
        kp = kpe_ref[pl.ds(off, tk), :]
        s = (_dot_nt(ql, kc) + _dot_nt(qp, kp)) * MLA_SCALE
        if masked:
            kpos = off + lax.broadcasted_iota(I32, (1, tk), 1)
            s = jnp.where(kpos <= qpos, s, -jnp.inf)
        m_old = m_ref[...]
        m_new = jnp.maximum(m_old, jnp.max(s, axis=-1, keepdims=True))
        corr = jnp.exp(m_old - m_new)
        p = jnp.exp(s - m_new)
        l_ref[...] = l_ref[...] * corr + jnp.sum(p, axis=-1, keepdims=True)
        acc_ref[...] = acc_ref[...] * corr + _dot(p.astype(BF16), kc)
        m_ref[...] = m_new

    def body(j, carry):
        step(j, False)
        return carry

    lax.fori_loop(0, qi, body, 0)
    step(qi, True)
    o_ref[...] = (acc_ref[...] / l_ref[...]).astype(o_ref.dtype)


def _mla_prompt_attention(qn, qpe, ckvb, kpeb, wuk, batch, t, tq, tk):
    assert tq == tk
    n = qn.shape[0]
    nq = t // tq
    grp = LANES // QK_ROPE
    return pl.pallas_call(
        functools.partial(_mla_prompt_kernel, tq=tq, tk=tk),
        grid=(batch, nq, MLA_HEADS),
        in_specs=[pl.BlockSpec((tq, LANES), lambda b, i, h: (b * nq + i, h)),
                  pl.BlockSpec((tq, LANES), lambda b, i, h: (b * nq + i, h // grp)),
                  pl.BlockSpec((t, KV_LORA), lambda b, i, h: (b, 0)),
                  pl.BlockSpec((t, LANES), lambda b, i, h: (b, 0)),
                  pl.BlockSpec((None, LANES, KV_LORA), lambda b, i, h: (h, 0, 0))],
        out_specs=pl.BlockSpec((tq, KV_LORA), lambda b, i, h: (b * nq + i, h)),
        out_shape=jax.ShapeDtypeStruct((n, MLA_HEADS * KV_LORA), BF16),
        scratch_shapes=[pltpu.VMEM((tq, 1), F32), pltpu.VMEM((tq, 1), F32), pltpu.VMEM((tq, KV_LORA), F32)],
        compiler_params=_params(("parallel", "parallel", "arbitrary")),
        name="mla_prompt_attention",
    )(qn, qpe, ckvb, kpeb, wuk)


def _qlat_kernel(qn_ref, wuk_ref, o_ref):
    o_ref[...] = _dot(qn_ref[...], wuk_ref[...])


def _mla_decode_kernel(pt_ref, ql_ref, qp_ref, cn_ref, kn_ref, *refs, g):
    ck_refs = refs[:g]
    kp_refs = refs[g:2 * g]
    o_ref, m_ref, l_ref, acc_ref, kbuf_ref, pbuf_ref = refs[2 * g:]
    s_id = pl.program_id(1)

    @pl.when(s_id == 0)
    def _():
        m_ref[...] = jnp.full_like(m_ref, -1e30)
        l_ref[...] = jnp.zeros_like(l_ref)
        acc_ref[...] = jnp.zeros_like(acc_ref)

    for p in range(g):
        kbuf_ref[p * PAGE_SIZE:(p + 1) * PAGE_SIZE, :] = ck_refs[p][...].astype(BF16)
        pbuf_ref[p * PAGE_SIZE:(p + 1) * PAGE_SIZE, :] = kp_refs[p][...].astype(BF16)
    ql = ql_ref[...]
    qp = qp_ref[...]
    kc = kbuf_ref[...]
    s = (_dot_nt(ql.astype(BF16), kc) + _dot_nt(qp, pbuf_ref[...])) * MLA_SCALE
    m_old = m_ref[...]
    m_new = jnp.maximum(m_old, jnp.max(s, axis=-1, keepdims=True))
    corr = jnp.exp(m_old - m_new)
    pr = jnp.exp(s - m_new)
    l_new = l_ref[...] * corr + jnp.sum(pr, axis=-1, keepdims=True)
    acc_new = acc_ref[...] * corr + _dot(pr.astype(BF16), kc)
    m_ref[...] = m_new
    l_ref[...] = l_new
    acc_ref[...] = acc_new

    @pl.when(s_id == pl.num_programs(1) - 1)
    def _():
        cn = cn_ref[...]
        sn = (jnp.sum(ql * cn, axis=-1, keepdims=True)
              + jnp.sum(qp.astype(F32) * kn_ref[...], axis=-1, keepdims=True)) * MLA_SCALE
        m_fin = jnp.maximum(m_new, sn)
        c2 = jnp.exp(m_new - m_fin)
        pn = jnp.exp(sn - m_fin)
        l_fin = l_new * c2 + pn
        o_ref[...] = ((acc_new * c2 + pn * cn) / l_fin).astype(o_ref.dtype)


def _mla_decode_attention(qlat, qpe, ckv_new, kpe_new, cache_ckv, cache_kpe, page_table, g=32):
    bsz, nh, _ = qlat.shape
    n_pages = page_table.shape[1]
    steps = n_pages // g
    per_b = lambda b, s, pt: (b, 0, 0)

    def page(p):
        return lambda b, s, pt: (pt[b, s * g + p], 0, 0)

    specs = [pl.BlockSpec((None, nh, KV_LORA), per_b), pl.BlockSpec((None, nh, QK_ROPE), per_b),
             pl.BlockSpec((None, 1, KV_LORA), per_b), pl.BlockSpec((None, 1, QK_ROPE), per_b)]
    specs += [pl.BlockSpec((None, PAGE_SIZE, KV_LORA), page(p)) for p in range(g)]
    specs += [pl.BlockSpec((None, PAGE_SIZE, QK_ROPE), page(p)) for p in range(g)]
    return pl.pallas_call(
        functools.partial(_mla_decode_kernel, g=g),
        grid_spec=pltpu.PrefetchScalarGridSpec(
            num_scalar_prefetch=1,
            grid=(bsz, steps),
            in_specs=specs,
            out_specs=pl.BlockSpec((None, nh, KV_LORA), per_b),
            scratch_shapes=[pltpu.VMEM((nh, 1), F32), pltpu.VMEM((nh, 1), F32), pltpu.VMEM((nh, KV_LORA), F32),
                            pltpu.VMEM((g * PAGE_SIZE, KV_LORA), BF16), pltpu.VMEM((g * PAGE_SIZE, QK_ROPE), BF16)]),
        out_shape=jax.ShapeDtypeStruct((bsz, nh, KV_LORA), BF16),
        compiler_params=_params(("parallel", "arbitrary")),
        name="mla_decode_attention",
    )(page_table, qlat, qpe, ckv_new, kpe_new, *([cache_ckv] * g), *([cache_kpe] * g))


def _rope_tables(pos):
    half = QK_ROPE // 2
    inv = ROPE_THETA ** (-jnp.arange(half, dtype=F32) / half)
    ang = pos.astype(F32)[:, None] * inv[None, :]
    cos = jnp.cos(ang)
    sin = jnp.sin(ang)
    cos_f = jnp.concatenate([cos, cos], axis=-1)
    sin_f = jnp.concatenate([-sin, sin], axis=-1)
    return jnp.tile(cos_f, (1, MLA_HEADS)), jnp.tile(sin_f, (1, MLA_HEADS))


def _swap_halves(w):
    half = w.shape[-1] // 2
    return jnp.concatenate([w[..., half:], w[..., :half]], axis=-1)


def kernel(x_prompt, x_sample, state_hgrn, cache_ckv, cache_kpe, page_table, hgrn_w_in, hgrn_lb_logits, hgrn_norm_g, hgrn_w_out, mla_w_dkv, mla_kv_norm_g, mla_w_ukv, mla_w_dq, mla_q_norm_g, mla_w_uq, mla_w_o, peer_w_q, peer_k1, peer_k2, peer_u, peer_v, ln_g, ln_b):
    bp, tp, d = x_prompt.shape
    bs, ts, _ = x_sample.shape
    nh = MLA_HEADS
    past_len = page_table.shape[1] * cache_ckv.shape[1]

    w_in = hgrn_w_in.astype(BF16)
    w_out = hgrn_w_out.astype(BF16)
    wq_peer = peer_w_q.astype(BF16)
    k1 = peer_k1.astype(BF16)
    k2 = peer_k2.astype(BF16)
    u_b = peer_u.astype(BF16)
    vt_b = [_transpose_bf16(peer_v[l], 512) for l in range(DEPTH)]
    w_ckv = mla_w_dkv[:, :KV_LORA]
    w_kr = mla_w_dkv[:, KV_LORA:]
    rep = LANES // QK_ROPE
    wkv = jnp.concatenate([w_ckv, jnp.tile(w_kr, (1, rep)), jnp.tile(_swap_halves(w_kr), (1, rep))], axis=1).astype(BF16)
    w_uk = mla_w_ukv[..., :QK_NOPE]
    w_uv = mla_w_ukv[..., QK_NOPE:]
    wuk_p = jnp.pad(jnp.transpose(w_uk, (1, 2, 0)), ((0, 0), (0, LANES - QK_NOPE), (0, 0))).astype(BF16)
    eye = jnp.eye(nh, dtype=F32)
    wuv_bd = (jnp.transpose(w_uv, (1, 0, 2))[:, :, None, :] * eye[:, None, :, None]).reshape(nh * KV_LORA, nh * V_DIM).astype(BF16)

    def mla_q_weights(j):
        wu = mla_w_uq[j].reshape(Q_LORA, nh, QK_NOPE + QK_ROPE)
        nope = jnp.pad(wu[..., :QK_NOPE], ((0, 0), (0, 0), (0, LANES - QK_NOPE))).reshape(Q_LORA, nh * LANES)
        rope = wu[..., QK_NOPE:]
        return jnp.concatenate([nope, rope.reshape(Q_LORA, nh * QK_ROPE),
                                _swap_halves(rope).reshape(Q_LORA, nh * QK_ROPE)], axis=1).astype(BF16)

    def trunk(x, batch, t, pos, s0_of, attend):
        n = batch * t
        tm = min(n, 512)
        states = []
        ckv = kpe = None
        ckvb = kpeb = None
        for l in range(DEPTH):
            if l < N_A_LAYERS:
                proj = _matmul(x, w_in[l], tm, 512)
                og, s_new = _hgrn_mixer(proj, batch, t, hgrn_lb_logits, hgrn_norm_g[l], s0_of(l), l)
                states.append(s_new)
                x = _matmul_residual_ln(og, w_out[l], x, ln_g[l, 0], ln_b[l, 0], min(tm, 256))
            else:
                j = l - N_A_LAYERS
                cos, sin = _rope_tables(pos)
                outs = _mla_prep(x, wkv, mla_w_dq[j].astype(BF16), mla_q_weights(j), mla_kv_norm_g,
                                 mla_q_norm_g[j], cos, sin, min(tm, 256))
                ckv_j, ckvb_j, kpe_j, kpeb_j, qn, qpe = outs
                if j == 0:
                    ckv, ckvb, kpe, kpeb = ckv_j, ckvb_j, kpe_j, kpeb_j
                o_lat = attend(qn, qpe, ckv, kpe, ckvb, kpeb)
                x = _matmul_residual_ln(o_lat, mla_w_o[j].astype(BF16), x, ln_g[l, 0], ln_b[l, 0],
                                        min(tm, 256), w0=wuv_bd)
            x = _peer_layer(x, wq_peer[l], k1[l], k2[l], u_b[l], vt_b[l], ln_g[l, 1], ln_b[l, 1], tm)
        return x, jnp.stack(states), ckv, kpe[:, :QK_ROPE]

    def attend_prompt(qn, qpe, ckv, kpe, ckvb, kpeb):
        return _mla_prompt_attention(qn, qpe, ckvb, kpeb, wuk_p, bp, tp, 512, 512)

    def attend_sample(qn, qpe, ckv, kpe, ckvb, kpeb):
        n = qn.shape[0]
        qlat = pl.pallas_call(
            _qlat_kernel,
            grid=(nh,),
            in_specs=[pl.BlockSpec((n, LANES), lambda h: (0, h)),
                      pl.BlockSpec((None, LANES, KV_LORA), lambda h: (h, 0, 0))],
            out_specs=pl.BlockSpec((None, n, KV_LORA), lambda h: (h, 0, 0)),
            out_shape=jax.ShapeDtypeStruct((nh, n, KV_LORA), F32),
            compiler_params=_params(("parallel",)),
            name="mla_q_latent",
        )(qn, wuk_p)
        qlat = jnp.transpose(qlat, (1, 0, 2))
        o = _mla_decode_attention(qlat, qpe.reshape(n, nh, QK_ROPE), ckv.reshape(n, 1, KV_LORA),
                                  kpe[:, :QK_ROPE].reshape(n, 1, QK_ROPE), cache_ckv, cache_kpe, page_table)
        return o.reshape(n, nh * KV_LORA)

    y_p, st_p, ckv_p, kpe_p = trunk(x_prompt.reshape(bp * tp, d), bp, tp, jnp.arange(tp),
                                    lambda l: None, attend_prompt)
    y_s, st_s, ckv_s, kpe_s = trunk(x_sample.reshape(bs * ts, d), bs, ts, past_len + jnp.arange(ts),
                                    lambda l: state_hgrn[l], attend_sample)
    return (y_p.reshape(bp, tp, d), y_s.reshape(bs, ts, d), st_p, st_s,
            ckv_p.reshape(bp, tp, KV_LORA), kpe_p.reshape(bp, tp, QK_ROPE),
            ckv_s.reshape(bs, ts, KV_LORA), kpe_s.reshape(bs, ts, QK_ROPE))
```

```python
import functools
import math

import numpy as np
import jax
import jax.numpy as jnp
from jax import lax
from jax.experimental import pallas as pl
from jax.experimental.pallas import tpu as pltpu

F32 = jnp.float32
BF16 = jnp.bfloat16
I32 = jnp.int32
I16 = jnp.int16

D_MODEL = 1024
DEPTH = 2
N_A_LAYERS = DEPTH // 2
HGRN_HEADS = 8
HGRN_DK = 128
HGRN_DV = 128
MLA_HEADS = 16
QK_NOPE = 64
QK_ROPE = 32
V_DIM = 64
KV_LORA = 256
Q_LORA = 384
ROPE_THETA = 10000.0
MLA_SCALE = (QK_NOPE + QK_ROPE) ** -0.5
PAGE_SIZE = 128
PEER_HEADS = 8
PEER_KEY_DIM = 256
N_KEYS = 128
PEER_TOPK = 16
DEEPNORM_ALPHA = (2.0 * DEPTH) ** 0.25
LN_EPS = 1e-5
RMS_EPS = 1e-6

LANES = 128
SUBLANES = 8
PACKED_ROWS = 16
VMEM_LIMIT_BYTES = 56 * 1024 * 1024


def _params(sem):
    return pltpu.CompilerParams(dimension_semantics=sem, vmem_limit_bytes=VMEM_LIMIT_BYTES)


def _dot(a, b):
    return jnp.dot(a, b, preferred_element_type=F32)


def _dot_nt(a, b):
    return lax.dot_general(a, b, (((1,), (1,)), ((), ())), preferred_element_type=F32)


def _dot_tn(a, b):
    return lax.dot_general(a, b, (((0,), (0,)), ((), ())), preferred_element_type=F32)


def _layernorm(x, g, b):
    mu = jnp.mean(x, axis=-1, keepdims=True)
    xc = x - mu
    var = jnp.mean(xc * xc, axis=-1, keepdims=True)
    return xc * lax.rsqrt(var + LN_EPS) * g + b


def _rmsnorm(x, g):
    return x * lax.rsqrt(jnp.mean(x * x, axis=-1, keepdims=True) + RMS_EPS) * g


def _mm_kernel(x_ref, w_ref, o_ref):
    o_ref[...] = _dot(x_ref[...].astype(BF16), w_ref[...]).astype(o_ref.dtype)


def _matmul(x, w, tm, tn, out_dtype=F32):
    m, k = x.shape
    n = w.shape[1]
    return pl.pallas_call(
        _mm_kernel,
        grid=(m // tm, n // tn),
        in_specs=[pl.BlockSpec((tm, k), lambda i, j: (i, 0)),
                  pl.BlockSpec((k, tn), lambda i, j: (0, j))],
        out_specs=pl.BlockSpec((tm, tn), lambda i, j: (i, j)),
        out_shape=jax.ShapeDtypeStruct((m, n), out_dtype),
        compiler_params=_params(("parallel", "parallel")),
        name="matmul",
    )(x, w)


def _mm_ln_kernel(*refs, two_stage):
    if two_stage:
        h_ref, w0_ref, w_ref, x_ref, g_ref, b_ref, o_ref = refs
        h = _dot(h_ref[...].astype(BF16), w0_ref[...]).astype(BF16)
    else:
        h_ref, w_ref, x_ref, g_ref, b_ref, o_ref = refs
        h = h_ref[...].astype(BF16)
    y = DEEPNORM_ALPHA * x_ref[...] + _dot(h, w_ref[...])
    o_ref[...] = _layernorm(y, g_ref[...], b_ref[...])


def _matmul_residual_ln(h, w, x, g, b, tm, w0=None):
    m, k = h.shape
    d = w.shape[1]
    row = lambda i: (i, 0)
    fixed = lambda i: (0, 0)
    ins = [h] + ([w0] if w0 is not None else []) + [w, x, g.reshape(1, d), b.reshape(1, d)]
    specs = [pl.BlockSpec((tm, k), row)]
    if w0 is not None:
        specs.append(pl.BlockSpec(w0.shape, fixed))
    specs += [pl.BlockSpec(w.shape, fixed), pl.BlockSpec((tm, d), row),
              pl.BlockSpec((1, d), fixed), pl.BlockSpec((1, d), fixed)]
    return pl.pallas_call(
        functools.partial(_mm_ln_kernel, two_stage=w0 is not None),
        grid=(m // tm,),
        in_specs=specs,
        out_specs=pl.BlockSpec((tm, d), row),
        out_shape=jax.ShapeDtypeStruct((m, d), F32),
        compiler_params=_params(("parallel",)),
        name="matmul_residual_ln",
    )(*ins)


def _hgrn_level_matrix(c):
    t = np.arange(c)[:, None]
    s = np.arange(c)[None, :]
    mats = [(s <= t)]
    m = c // 2
    while m >= 1:
        bd = (t // (2 * m)) * (2 * m) + m - 1
        mats.append(s <= bd)
        m //= 2
    return np.concatenate(mats, axis=0).astype(np.float32)


def _hgrn_head(q, fz, v, lbl, ng, lvl, st, *, c, n_real, layer):
    dk = q.shape[-1]
    e = jnp.exp(lbl - jnp.max(lbl, axis=0, keepdims=True))
    sm = e / jnp.sum(e, axis=0, keepdims=True)
    lb = jnp.sum(sm[:layer + 1], axis=0, keepdims=True)

    logf = jnp.log(lb + (1.0 - lb) * jax.nn.sigmoid(fz))
    k = (1.0 - lb) * jax.nn.sigmoid(-fz)
    row_id = lax.broadcasted_iota(I32, (c, 1), 0)
    if n_real != c:
        live = row_id < n_real
        logf = jnp.where(live, logf, 0.0)
        k = jnp.where(live, k, 0.0)

    hi = logf.astype(BF16)
    r1 = logf - hi.astype(F32)
    mid = r1.astype(BF16)
    lo = (r1 - mid.astype(F32)).astype(BF16)
    parts = _dot(lvl, jnp.concatenate([hi, mid, lo], axis=1))
    sums = parts[:, :dk] + (parts[:, dk:2 * dk] + parts[:, 2 * dk:])
    b = sums[:c]

    kb = k.astype(BF16)
    vb = v.astype(BF16)
    col_id = lax.broadcasted_iota(I32, (1, c), 1)
    att = jnp.where(row_id == col_id, _dot_nt(q.astype(BF16), kb), 0.0)
    m = c // 2
    level = 1
    while m >= 1:
        r = sums[level * c:(level + 1) * c]
        second = (row_id & (2 * m - 1)) >= m
        ql = q * jnp.exp(jnp.where(second, b - r, -jnp.inf))
        kl = k * jnp.exp(jnp.where(second, -jnp.inf, r - b))
        a = _dot_nt(ql.astype(BF16), kl.astype(BF16))
        if 2 * m == c:
            att = att + a
        else:
            sh = int(math.log2(2 * m))
            att = att + jnp.where((row_id >> sh) == (col_id >> sh), a, 0.0)
        m //= 2
        level += 1

    o = _dot_nt((q * jnp.exp(b)).astype(BF16), st.astype(BF16)) + _dot(att.astype(BF16), vb)
    b_last = b[c - 1:c, :]
    st_new = st * jnp.exp(b_last) + _dot_tn(vb, (k * jnp.exp(b_last - b)).astype(BF16))
    return _rmsnorm(o, ng), st_new


def _hgrn_kernel(*refs, c, n_real, layer, has_s0, hg):
    if has_s0:
        (q_ref, f_ref, i_ref, g_ref, lbl_ref, ng_ref, lvl_ref, s0_ref, o_ref, s_ref, st_ref) = refs
    else:
        (q_ref, f_ref, i_ref, g_ref, lbl_ref, ng_ref, lvl_ref, o_ref, s_ref, st_ref) = refs
    hb = pl.program_id(1)
    ci = pl.program_id(2)
    dk = HGRN_DK
    rows = q_ref.shape[0]

    @pl.when(ci == 0)
    def _():
        for hh in range(hg):
            st_ref[hh] = s0_ref[hh].T if has_s0 else jnp.zeros((HGRN_DV, HGRN_DK), F32)

    def rows_of(ref, cols):
        x = ref[:, cols]
        return jnp.broadcast_to(x, (c, dk)) if rows != c else x

    for hh in range(hg):
        cols = slice(hh * dk, (hh + 1) * dk)
        ng = ng_ref[pl.ds(hb * hg + hh, 1), :]
        o, st_new = _hgrn_head(rows_of(q_ref, cols), rows_of(f_ref, cols), rows_of(i_ref, cols),
                               lbl_ref[:, cols], ng, lvl_ref[...], st_ref[hh], c=c, n_real=n_real, layer=layer)
        st_ref[hh] = st_new
        g = g_ref[:, cols]
        o_ref[:, cols] = o[:rows] * (g * jax.nn.sigmoid(g))

        @pl.when(ci == pl.num_programs(2) - 1)
        def _(hh=hh, st_new=st_new):
            s_ref[hh] = st_new.T


def _hgrn_mixer(proj, batch, t, lb_logits, norm_g, s0, layer):
    n = proj.shape[0]
    hd = HGRN_HEADS * HGRN_DK
    nh = HGRN_HEADS
    if t >= LANES:
        c, rows, n_real, hg = LANES, LANES, LANES, 1
        w = hg * HGRN_DK
        arr = proj
        col = lambda off: pl.BlockSpec((rows, w), lambda b, h, ci: (b * (t // c) + ci, off * (nh // hg) + h))
        o_spec = pl.BlockSpec((rows, w), lambda b, h, ci: (b * (t // c) + ci, h))
        o_shape = jax.ShapeDtypeStruct((n, nh * HGRN_DV), F32)
        nchunk = t // c
    else:
        assert t == 1
        c, rows, n_real, hg = 2 * SUBLANES, 1, 1, nh
        w = hg * HGRN_DK
        arr = proj.reshape(n, 1, 4 * hd)
        col = lambda off: pl.BlockSpec((None, 1, w), lambda b, h, ci: (b, 0, off * (nh // hg) + h))
        o_spec = pl.BlockSpec((None, 1, w), lambda b, h, ci: (b, 0, h))
        o_shape = jax.ShapeDtypeStruct((n, 1, nh * HGRN_DV), F32)
        nchunk = 1
    lvl = jnp.asarray(_hgrn_level_matrix(c), BF16)
    nl = lb_logits.shape[0]
    state_spec = pl.BlockSpec((None, hg, HGRN_DK, HGRN_DV), lambda b, h, ci: (b, h, 0, 0))
    ins = [arr, arr, arr, arr, lb_logits, norm_g, lvl]
    specs = [col(0), col(1), col(2), col(3),
             pl.BlockSpec((nl, w), lambda b, h, ci: (0, h)),
             pl.BlockSpec((nh, HGRN_DV), lambda b, h, ci: (0, 0)),
             pl.BlockSpec(lvl.shape, lambda b, h, ci: (0, 0))]
    if s0 is not None:
        ins.append(s0)
        specs.append(state_spec)
    o, s = pl.pallas_call(
        functools.partial(_hgrn_kernel, c=c, n_real=n_real, layer=layer, has_s0=s0 is not None, hg=hg),
        grid=(batch, nh // hg, nchunk),
        in_specs=specs,
        out_specs=[o_spec, state_spec],
        out_shape=[o_shape, jax.ShapeDtypeStruct((batch, nh, HGRN_DK, HGRN_DV), F32)],
        scratch_shapes=[pltpu.VMEM((hg, HGRN_DV, HGRN_DK), F32)],
        compiler_params=_params(("parallel", "parallel", "arbitrary")),
        name="hgrn2_chunk",
    )(*ins)
    return o.reshape(n, nh * HGRN_DV), s


def _top16(s, val_ref):
    n = s.shape[0]
    rid = lax.broadcasted_iota(I32, s.shape, 0)
    rank = jnp.full(s.shape, PEER_TOPK, I32)
    for kk in range(PEER_TOPK):
        mx = jnp.max(s, axis=0, keepdims=True)
        ix = jnp.min(jnp.where(s == mx, rid, n), axis=0, keepdims=True)
        hit = rid == ix
        val_ref[kk:kk + 1, :] = mx
        rank = jnp.where(hit, kk, rank)
        s = jnp.where(hit, -jnp.inf, s)
    return rank


_PAIR_COUNT = [PEER_TOPK // (i + 1) for i in range(PEER_TOPK)]
_PAIR_START = [sum(_PAIR_COUNT[:i]) for i in range(PEER_TOPK)]
_N_PAIRS = sum(_PAIR_COUNT)
_N_PAIR_ROWS = -(-_N_PAIRS // SUBLANES) * SUBLANES


def _peer_topk_kernel(q_ref, k1_ref, k2_ref, inv1_ref, inv2_ref, p_ref, v1_ref, v2_ref, sv_ref, cand_ref, pt_ref):
    half = PEER_KEY_DIM // 2
    kk = PEER_TOPK
    for h in range(PEER_HEADS):
        q = q_ref[:, h * PEER_KEY_DIM:(h + 1) * PEER_KEY_DIM].astype(BF16)
        s1 = _dot_nt(k1_ref[h], q[:, :half])
        s2 = _dot_nt(k2_ref[h], q[:, half:])
        inv1_ref[h] = _top16(s1, v1_ref)
        inv2_ref[h] = _top16(s2, v2_ref).astype(I16)
        cand_ref[...] = jnp.full(cand_ref.shape, -jnp.inf, F32)
        for i in range(kk):
            cand_ref[_PAIR_START[i]:_PAIR_START[i] + _PAIR_COUNT[i], :] = v1_ref[i:i + 1, :] + v2_ref[0:_PAIR_COUNT[i], :]
        rank = _top16(cand_ref[...], sv_ref)
        sv = sv_ref[...]
        e = jnp.exp(sv - jnp.max(sv, axis=0, keepdims=True))
        gate = e / jnp.sum(e, axis=0, keepdims=True)
        p = jnp.zeros(rank.shape, F32)
        for j in range(kk):
            p = jnp.where(rank == j, gate[j:j + 1, :], p)
        pt_ref[...] = jnp.zeros(pt_ref.shape, F32)
        for i in range(kk):
            pt_ref[i * kk:i * kk + _PAIR_COUNT[i], :] = p[_PAIR_START[i]:_PAIR_START[i] + _PAIR_COUNT[i], :]
        p_ref[h] = pt_ref[...].astype(BF16)


def _peer_topk(q, k1, k2, tt):
    n = q.shape[0]
    nh = PEER_HEADS
    kk = PEER_TOPK
    return pl.pallas_call(
        _peer_topk_kernel,
        grid=(n // tt,),
        in_specs=[pl.BlockSpec((tt, nh * PEER_KEY_DIM), lambda i: (i, 0)),
                  pl.BlockSpec(k1.shape, lambda i: (0, 0, 0)),
                  pl.BlockSpec(k2.shape, lambda i: (0, 0, 0))],
        out_specs=[pl.BlockSpec((nh, N_KEYS, tt), lambda i: (0, 0, i)),
                   pl.BlockSpec((nh, N_KEYS, tt), lambda i: (0, 0, i)),
                   pl.BlockSpec((nh, kk * kk, tt), lambda i: (0, 0, i))],
        out_shape=[jax.ShapeDtypeStruct((nh, N_KEYS, n), I32),
                   jax.ShapeDtypeStruct((nh, N_KEYS, n), I16),
                   jax.ShapeDtypeStruct((nh, kk * kk, n), BF16)],
        scratch_shapes=[pltpu.VMEM((kk, tt), F32), pltpu.VMEM((kk, tt), F32), pltpu.VMEM((kk, tt), F32),
                        pltpu.VMEM((_N_PAIR_ROWS, tt), F32), pltpu.VMEM((kk * kk, tt), F32)],
        compiler_params=_params(("parallel",)),
        name="peer_topk",
    )(q, k1, k2)


_GATHER_ROWS = lax.GatherDimensionNumbers(offset_dims=(), collapsed_slice_dims=(0,), start_index_map=(0,),
                                          operand_batching_dims=(1,), start_indices_batching_dims=(1,))


def _gather_rows16(table, idx):
    return lax.gather(table, idx[..., None], _GATHER_ROWS, (1, 1), mode=lax.GatherScatterMode.PROMISE_IN_BOUNDS)


def _peer_dense_kernel(xt_ref, x_ref, u_ref, vt_ref, inv1_ref, inv2_ref, p_ref, g_ref, b_ref,
                       o_ref, ft_ref, prow_ref, zt_ref, act_ref, *, ag):
    a_blk = pl.program_id(1)
    kk = PEER_TOPK
    tm = xt_ref.shape[1]

    @pl.when(a_blk == 0)
    def _():
        ft_ref[...] = jnp.zeros_like(ft_ref)

    ht = _dot(u_ref[...], xt_ref[...])
    act_ref[...] = (0.5 * ht * (1.0 + lax.erf(ht * (1.0 / math.sqrt(2.0))))).astype(BF16)

    for aa in range(ag):
        a = a_blk * ag + aa
        for h in range(PEER_HEADS):
            i_row = jnp.broadcast_to(inv1_ref[h, pl.ds(a, 1), :].astype(I16), (kk, tm))
            rows = [p_ref[h, i * kk:(i + 1) * kk, :] for i in range(kk)]
            bit = 1
            while len(rows) > 1:
                odd = (i_row & bit) != 0
                rows = [jnp.where(odd, rows[2 * q + 1], rows[2 * q]) for q in range(len(rows) // 2)]
                bit *= 2
            prow_ref[aa, h] = jnp.where(i_row < kk, rows[0], jnp.zeros_like(rows[0]))

    for r in range(N_KEYS // PACKED_ROWS):
        accs = [jnp.zeros((PACKED_ROWS, tm), BF16) for _ in range(ag)]
        for h in range(PEER_HEADS):
            j = inv2_ref[h, r * PACKED_ROWS:(r + 1) * PACKED_ROWS, :]
            valid = j < kk
            jc = j & (kk - 1)
            for aa in range(ag):
                gv = _gather_rows16(prow_ref[aa, h], jc)
                accs[aa] = accs[aa] + jnp.where(valid, gv, jnp.zeros_like(gv))
        for aa in range(ag):
            lo = aa * N_KEYS + r * PACKED_ROWS
            zt_ref[lo:lo + PACKED_ROWS, :] = accs[aa] * act_ref[lo:lo + PACKED_ROWS, :]

    ft_ref[...] += _dot(vt_ref[...], zt_ref[...])

    @pl.when(a_blk == pl.num_programs(1) - 1)
    def _():
        y = DEEPNORM_ALPHA * x_ref[...] + ft_ref[...].T
        o_ref[...] = _layernorm(y, g_ref[...], b_ref[...])


def _peer_dense(x, xt, u, vt, inv1, inv2, p, g, b, tm, ag=8):
    n, d = x.shape
    ne = u.shape[0]
    te = ag * N_KEYS
    nh = PEER_HEADS
    kk = PEER_TOPK
    return pl.pallas_call(
        functools.partial(_peer_dense_kernel, ag=ag),
        grid=(n // tm, ne // te),
        in_specs=[pl.BlockSpec((d, tm), lambda i, a: (0, i)),
                  pl.BlockSpec((tm, d), lambda i, a: (i, 0)),
                  pl.BlockSpec((te, d), lambda i, a: (a, 0)),
                  pl.BlockSpec((d, te), lambda i, a: (0, a)),
                  pl.BlockSpec((nh, N_KEYS, tm), lambda i, a: (0, 0, i)),
                  pl.BlockSpec((nh, N_KEYS, tm), lambda i, a: (0, 0, i)),
                  pl.BlockSpec((nh, kk * kk, tm), lambda i, a: (0, 0, i)),
                  pl.BlockSpec((1, d), lambda i, a: (0, 0)),
                  pl.BlockSpec((1, d), lambda i, a: (0, 0))],
        out_specs=pl.BlockSpec((tm, d), lambda i, a: (i, 0)),
        out_shape=jax.ShapeDtypeStruct((n, d), F32),
        scratch_shapes=[pltpu.VMEM((d, tm), F32), pltpu.VMEM((ag, nh, kk, tm), BF16),
                        pltpu.VMEM((te, tm), BF16), pltpu.VMEM((te, tm), BF16)],
        compiler_params=_params(("parallel", "arbitrary")),
        name="peer_dense",
    )(xt, x, u, vt, inv1, inv2, p, g.reshape(1, d), b.reshape(1, d))


def _transpose_kernel(x_ref, o_ref):
    o_ref[...] = x_ref[...].T.astype(o_ref.dtype)


def _transpose_bf16(x, tm):
    n, d = x.shape
    return pl.pallas_call(
        _transpose_kernel,
        grid=(n // tm,),
        in_specs=[pl.BlockSpec((tm, d), lambda i: (i, 0))],
        out_specs=pl.BlockSpec((d, tm), lambda i: (0, i)),
        out_shape=jax.ShapeDtypeStruct((d, n), BF16),
        compiler_params=_params(("parallel",)),
        name="transpose_bf16",
    )(x)


def _peer_layer(x, wq, k1, k2, u, vt, g, b, tm):
    n = x.shape[0]
    q = _matmul(x, wq, min(tm, 512), wq.shape[1])
    inv1, inv2, p = _peer_topk(q, k1, k2, min(n, 2 * LANES))
    xt = _transpose_bf16(x, min(n, 512))
    return _peer_dense(x, xt, u, vt, inv1, inv2, p, g, b, tm)


def _mla_prep_kernel(x_ref, wkv_ref, wdq_ref, wq_ref, kvg_ref, qg_ref, cos_ref, sin_ref,
                     ckv_ref, ckvb_ref, kpe_ref, kpeb_ref, qn_ref, qpe_ref):
    xb = x_ref[...].astype(BF16)
    cos = cos_ref[...]
    sin = sin_ref[...]
    kv = _dot(xb, wkv_ref[...])
    ckv = _rmsnorm(kv[:, :KV_LORA], kvg_ref[...])
    ckv_ref[...] = ckv
    ckvb_ref[...] = ckv.astype(BF16)
    kr = kv[:, KV_LORA:KV_LORA + LANES]
    kr_sw = kv[:, KV_LORA + LANES:]
    kpe = kr * cos[:, :LANES] + kr_sw * sin[:, :LANES]
    kpe_ref[...] = kpe
    kpeb_ref[...] = kpe.astype(BF16)
    cq = _rmsnorm(_dot(xb, wdq_ref[...]), qg_ref[...]).astype(BF16)
    qa = _dot(cq, wq_ref[...])
    nn = qn_ref.shape[1]
    nr = qpe_ref.shape[1]
    qn_ref[...] = qa[:, :nn].astype(BF16)
    qpe_ref[...] = (qa[:, nn:nn + nr] * cos + qa[:, nn + nr:] * sin).astype(BF16)


def _mla_prep(x, wkv, wdq, wq, kvg, qg, cos, sin, tm):
    n, d = x.shape
    nn = MLA_HEADS * LANES
    nr = MLA_HEADS * QK_ROPE
    row = lambda i: (i, 0)
    fixed = lambda i: (0, 0)
    t = cos.shape[0]
    if t >= tm:
        rope_spec = pl.BlockSpec((tm, nr), lambda i: (i % (t // tm), 0))
    else:
        assert t == 1
        rope_spec = pl.BlockSpec((1, nr), fixed)
    return pl.pallas_call(
        _mla_prep_kernel,
        grid=(n // tm,),
        in_specs=[pl.BlockSpec((tm, d), row), pl.BlockSpec(wkv.shape, fixed), pl.BlockSpec(wdq.shape, fixed),
                  pl.BlockSpec(wq.shape, fixed), pl.BlockSpec((1, KV_LORA), fixed), pl.BlockSpec((1, Q_LORA), fixed),
                  rope_spec, rope_spec],
        out_specs=[pl.BlockSpec((tm, KV_LORA), row), pl.BlockSpec((tm, KV_LORA), row),
                   pl.BlockSpec((tm, LANES), row), pl.BlockSpec((tm, LANES), row),
                   pl.BlockSpec((tm, nn), row), pl.BlockSpec((tm, nr), row)],
        out_shape=[jax.ShapeDtypeStruct((n, KV_LORA), F32), jax.ShapeDtypeStruct((n, KV_LORA), BF16),
                   jax.ShapeDtypeStruct((n, LANES), F32), jax.ShapeDtypeStruct((n, LANES), BF16),
                   jax.ShapeDtypeStruct((n, nn), BF16), jax.ShapeDtypeStruct((n, nr), BF16)],
        compiler_params=_params(("parallel",)),
        name="mla_prep",
    )(x, wkv, wdq, wq, kvg.reshape(1, KV_LORA), qg.reshape(1, Q_LORA), cos, sin)


def _head_rope_lanes(qpe, h):
    lane = lax.broadcasted_iota(I32, (1, LANES), 1)
    return jnp.where((lane >> int(math.log2(QK_ROPE))) == (h & (LANES // QK_ROPE - 1)), qpe, jnp.zeros_like(qpe))


def _mla_prompt_kernel(qn_ref, qpe_ref, ckv_ref, kpe_ref, wuk_ref, o_ref, m_ref, l_ref, acc_ref, *, tq, tk):
    qi = pl.program_id(1)
    h = pl.program_id(2)
    ql = _dot(qn_ref[...], wuk_ref[...]).astype(BF16)
    qp = _head_rope_lanes(qpe_ref[...], h)
    m_ref[...] = jnp.full_like(m_ref, -jnp.inf)
    l_ref[...] = jnp.zeros_like(l_ref)
    acc_ref[...] = jnp.zeros_like(acc_ref)
    qpos = qi * tq + lax.broadcasted_iota(I32, (tq, 1), 0)

    def body(j, carry):
        off = pl.multiple_of(j * tk, tk)
        kc = ckv_ref[pl.ds(off, tk), :]
        kp = kpe_ref[pl.ds(off, tk), :]
        s = (_dot_nt(ql, kc) + _dot_nt(qp, kp)) * MLA_SCALE
        kpos = off + lax.broadcasted_iota(I32, (1, tk), 1)
        s = jnp.where(kpos <= qpos, s, -jnp.inf)
        m_old = m_ref[...]
        m_new = jnp.maximum(m_old, jnp.max(s, axis=-1, keepdims=True))
        corr = jnp.exp(m_old - m_new)
        p = jnp.exp(s - m_new)
        l_ref[...] = l_ref[...] * corr + jnp.sum(p, axis=-1, keepdims=True)
        acc_ref[...] = acc_ref[...] * corr + _dot(p.astype(BF16), kc)
        m_ref[...] = m_new
        return carry

    lax.fori_loop(0, qi + 1, body, 0)
    o_ref[...] = (acc_ref[...] / l_ref[...]).astype(o_ref.dtype)


def _mla_prompt_attention(qn, qpe, ckvb, kpeb, wuk, batch, t, tq, tk):
    assert tq == tk
    n = qn.shape[0]
    nq = t // tq
    grp = LANES // QK_ROPE
    return pl.pallas_call(
        functools.partial(_mla_prompt_kernel, tq=tq, tk=tk),
        grid=(batch, nq, MLA_HEADS),
        in_specs=[pl.BlockSpec((tq, LANES), lambda b, i, h: (b * nq + i, h)),
                  pl.BlockSpec((tq, LANES), lambda b, i, h: (b * nq + i, h // grp)),
                  pl.BlockSpec((t, KV_LORA), lambda b, i, h: (b, 0)),
                  pl.BlockSpec((t, LANES), lambda b, i, h: (b, 0)),
                  pl.BlockSpec((None, LANES, KV_LORA), lambda b, i, h: (h, 0, 0))],
        out_specs=pl.BlockSpec((tq, KV_LORA), lambda b, i, h: (b * nq + i, h)),
        out_shape=jax.ShapeDtypeStruct((n, MLA_HEADS * KV_LORA), BF16),
        scratch_shapes=[pltpu.VMEM((tq, 1), F32), pltpu.VMEM((tq, 1), F32), pltpu.VMEM((tq, KV_LORA), F32)],
        compiler_params=_params(("parallel", "parallel", "arbitrary")),
        name="mla_prompt_attention",
    )(qn, qpe, ckvb, kpeb, wuk)


def _qlat_kernel(qn_ref, wuk_ref, o_ref):
    o_ref[...] = _dot(qn_ref[...], wuk_ref[...])


def _mla_decode_kernel(pt_ref, ql_ref, qp_ref, cn_ref, kn_ref, *refs, g):
    ck_refs = refs[:g]
    kp_refs = refs[g:2 * g]
    o_ref, m_ref, l_ref, acc_ref, kbuf_ref, pbuf_ref = refs[2 * g:]
    s_id = pl.program_id(1)

    @pl.when(s_id == 0)
    def _():
        m_ref[...] = jnp.full_like(m_ref, -1e30)
        l_ref[...] = jnp.zeros_like(l_ref)
        acc_ref[...] = jnp.zeros_like(acc_ref)

    for p in range(g):
        kbuf_ref[p * PAGE_SIZE:(p + 1) * PAGE_SIZE, :] = ck_refs[p][...].astype(BF16)
        pbuf_ref[:, p * PAGE_SIZE:(p + 1) * PAGE_SIZE] = kp_refs[p][...].astype(BF16)
    ql = ql_ref[...]
    qp = qp_ref[...]
    kc = kbuf_ref[...]
    s = (_dot_nt(ql.astype(BF16), kc) + _dot(qp, pbuf_ref[...])) * MLA_SCALE
    m_old = m_ref[...]
    m_new = jnp.maximum(m_old, jnp.max(s, axis=-1, keepdims=True))
    corr = jnp.exp(m_old - m_new)
    pr = jnp.exp(s - m_new)
    l_new = l_ref[...] * corr + jnp.sum(pr, axis=-1, keepdims=True)
    acc_new = acc_ref[...] * corr + _dot(pr.astype(BF16), kc)
    m_ref[...] = m_new
    l_ref[...] = l_new
    acc_ref[...] = acc_new

    @pl.when(s_id == pl.num_programs(1) - 1)
    def _():
        cn = cn_ref[...]
        sn = (jnp.sum(ql * cn, axis=-1, keepdims=True)
              + jnp.sum(qp.astype(F32) * kn_ref[...], axis=-1, keepdims=True)) * MLA_SCALE
        m_fin = jnp.maximum(m_new, sn)
        c2 = jnp.exp(m_new - m_fin)
        pn = jnp.exp(sn - m_fin)
        l_fin = l_new * c2 + pn
        o_ref[...] = ((acc_new * c2 + pn * cn) / l_fin).astype(o_ref.dtype)


def _mla_decode_attention(qlat, qpe, ckv_new, kpe_new, cache_ckv, cache_kpe_t, page_table, g=32):
    bsz, nh, _ = qlat.shape
    n_pages = page_table.shape[1]
    steps = n_pages // g
    per_b = lambda b, s, pt: (b, 0, 0)

    def page(p):
        return lambda b, s, pt: (pt[b, s * g + p], 0, 0)

    specs = [pl.BlockSpec((None, nh, KV_LORA), per_b), pl.BlockSpec((None, nh, QK_ROPE), per_b),
             pl.BlockSpec((None, 1, KV_LORA), per_b), pl.BlockSpec((None, 1, QK_ROPE), per_b)]
    specs += [pl.BlockSpec((None, PAGE_SIZE, KV_LORA), page(p)) for p in range(g)]
    specs += [pl.BlockSpec((None, QK_ROPE, PAGE_SIZE), page(p)) for p in range(g)]
    return pl.pallas_call(
        functools.partial(_mla_decode_kernel, g=g),
        grid_spec=pltpu.PrefetchScalarGridSpec(
            num_scalar_prefetch=1,
            grid=(bsz, steps),
            in_specs=specs,
            out_specs=pl.BlockSpec((None, nh, KV_LORA), per_b),
            scratch_shapes=[pltpu.VMEM((nh, 1), F32), pltpu.VMEM((nh, 1), F32), pltpu.VMEM((nh, KV_LORA), F32),
                            pltpu.VMEM((g * PAGE_SIZE, KV_LORA), BF16), pltpu.VMEM((QK_ROPE, g * PAGE_SIZE), BF16)]),
        out_shape=jax.ShapeDtypeStruct((bsz, nh, KV_LORA), BF16),
        compiler_params=_params(("parallel", "arbitrary")),
        name="mla_decode_attention",
    )(page_table, qlat, qpe, ckv_new, kpe_new, *([cache_ckv] * g), *([cache_kpe_t] * g))


def _rope_tables(pos):
    half = QK_ROPE // 2
    inv = ROPE_THETA ** (-jnp.arange(half, dtype=F32) / half)
    ang = pos.astype(F32)[:, None] * inv[None, :]
    cos = jnp.cos(ang)
    sin = jnp.sin(ang)
    cos_f = jnp.concatenate([cos, cos], axis=-1)
    sin_f = jnp.concatenate([-sin, sin], axis=-1)
    return jnp.tile(cos_f, (1, MLA_HEADS)), jnp.tile(sin_f, (1, MLA_HEADS))


def _swap_halves(w):
    half = w.shape[-1] // 2
    return jnp.concatenate([w[..., half:], w[..., :half]], axis=-1)


def kernel(x_prompt, x_sample, state_hgrn, cache_ckv, cache_kpe, page_table, hgrn_w_in, hgrn_lb_logits, hgrn_norm_g, hgrn_w_out, mla_w_dkv, mla_kv_norm_g, mla_w_ukv, mla_w_dq, mla_q_norm_g, mla_w_uq, mla_w_o, peer_w_q, peer_k1, peer_k2, peer_u, peer_v, ln_g, ln_b):
    bp, tp, d = x_prompt.shape
    bs, ts, _ = x_sample.shape
    nh = MLA_HEADS
    past_len = page_table.shape[1] * cache_ckv.shape[1]

    w_in = hgrn_w_in.astype(BF16)
    w_out = hgrn_w_out.astype(BF16)
    wq_peer = peer_w_q.astype(BF16)
    k1 = peer_k1.astype(BF16)
    k2 = peer_k2.astype(BF16)
    u_b = peer_u.astype(BF16)
    vt_b = [_transpose_bf16(peer_v[l], 512) for l in range(DEPTH)]
    w_ckv = mla_w_dkv[:, :KV_LORA]
    w_kr = mla_w_dkv[:, KV_LORA:]
    rep = LANES // QK_ROPE
    wkv = jnp.concatenate([w_ckv, jnp.tile(w_kr, (1, rep)), jnp.tile(_swap_halves(w_kr), (1, rep))], axis=1).astype(BF16)
    w_uk = mla_w_ukv[..., :QK_NOPE]
    w_uv = mla_w_ukv[..., QK_NOPE:]
    wuk_p = jnp.pad(jnp.transpose(w_uk, (1, 2, 0)), ((0, 0), (0, LANES - QK_NOPE), (0, 0))).astype(BF16)
    eye = jnp.eye(nh, dtype=F32)
    wuv_bd = (jnp.transpose(w_uv, (1, 0, 2))[:, :, None, :] * eye[:, None, :, None]).reshape(nh * KV_LORA, nh * V_DIM).astype(BF16)

    def mla_q_weights(j):
        wu = mla_w_uq[j].reshape(Q_LORA, nh, QK_NOPE + QK_ROPE)
        nope = jnp.pad(wu[..., :QK_NOPE], ((0, 0), (0, 0), (0, LANES - QK_NOPE))).reshape(Q_LORA, nh * LANES)
        rope = wu[..., QK_NOPE:]
        return jnp.concatenate([nope, rope.reshape(Q_LORA, nh * QK_ROPE),
                                _swap_halves(rope).reshape(Q_LORA, nh * QK_ROPE)], axis=1).astype(BF16)

    def trunk(x, batch, t, pos, s0_of, attend):
        n = batch * t
        tm = min(n, 512)
        states = []
        ckv = kpe = None
        ckvb = kpeb = None
        for l in range(DEPTH):
            if l < N_A_LAYERS:
                proj = _matmul(x, w_in[l], tm, w_in.shape[2])
                og, s_new = _hgrn_mixer(proj, batch, t, hgrn_lb_logits, hgrn_norm_g[l], s0_of(l), l)
                states.append(s_new)
                x = _matmul_residual_ln(og, w_out[l], x, ln_g[l, 0], ln_b[l, 0], min(tm, 256))
            else:
                j = l - N_A_LAYERS
                cos, sin = _rope_tables(pos)
                outs = _mla_prep(x, wkv, mla_w_dq[j].astype(BF16), mla_q_weights(j), mla_kv_norm_g,
                                 mla_q_norm_g[j], cos, sin, min(tm, 256))
                ckv_j, ckvb_j, kpe_j, kpeb_j, qn, qpe = outs
                if j == 0:
                    ckv, ckvb, kpe, kpeb = ckv_j, ckvb_j, kpe_j, kpeb_j
                o_lat = attend(qn, qpe, ckv, kpe, ckvb, kpeb)
                x = _matmul_residual_ln(o_lat, mla_w_o[j].astype(BF16), x, ln_g[l, 0], ln_b[l, 0],
                                        min(tm, 256), w0=wuv_bd)
            x = _peer_layer(x, wq_peer[l], k1[l], k2[l], u_b[l], vt_b[l], ln_g[l, 1], ln_b[l, 1], tm)
        return x, jnp.stack(states), ckv, kpe[:, :QK_ROPE]

    def attend_prompt(qn, qpe, ckv, kpe, ckvb, kpeb):
        return _mla_prompt_attention(qn, qpe, ckvb, kpeb, wuk_p, bp, tp, 512, 512)

    def attend_sample(qn, qpe, ckv, kpe, ckvb, kpeb):
        n = qn.shape[0]
        qlat = pl.pallas_call(
            _qlat_kernel,
            grid=(nh,),
            in_specs=[pl.BlockSpec((n, LANES), lambda h: (0, h)),
                      pl.BlockSpec((None, LANES, KV_LORA), lambda h: (h, 0, 0))],
            out_specs=pl.BlockSpec((None, n, KV_LORA), lambda h: (h, 0, 0)),
            out_shape=jax.ShapeDtypeStruct((nh, n, KV_LORA), F32),
            compiler_params=_params(("parallel",)),
            name="mla_q_latent",
        )(qn, wuk_p)
        qlat = jnp.transpose(qlat, (1, 0, 2))
        cache_kpe_t = jnp.transpose(cache_kpe, (0, 2, 1))
        o = _mla_decode_attention(qlat, qpe.reshape(n, nh, QK_ROPE), ckv.reshape(n, 1, KV_LORA),
                                  kpe[:, :QK_ROPE].reshape(n, 1, QK_ROPE), cache_ckv, cache_kpe_t, page_table)
        return o.reshape(n, nh * KV_LORA)

    y_p, st_p, ckv_p, kpe_p = trunk(x_prompt.reshape(bp * tp, d), bp, tp, jnp.arange(tp),
                                    lambda l: None, attend_prompt)
    y_s, st_s, ckv_s, kpe_s = trunk(x_sample.reshape(bs * ts, d), bs, ts, past_len + jnp.arange(ts),
                                    lambda l: state_hgrn[l], attend_sample)
    return (y_p.reshape(bp, tp, d), y_s.reshape(bs, ts, d), st_p, st_s,
            ckv_p.reshape(bp, tp, KV_LORA), kpe_p.reshape(bp, tp, QK_ROPE),
            ckv_s.reshape(bs, ts, KV_LORA), kpe_s.reshape(bs, ts, QK_ROPE))
```
